```python
import math
import jax
import jax.numpy as jnp
from jax import lax
import numpy as np

D_MODEL = 1024
BATCH = 4
SEQ = 4096
DEPTH = 4
DEC_BATCH = 128
DEC_SEQ = 4
PAST_LEN = 2048
PAGE_SIZE = 128

HEAD_DIM = 64
N_EVEN = (DEPTH + 1) // 2
N_ODD = DEPTH // 2
H_GDN = D_MODEL // 2 // HEAD_DIM
W_GDN = H_GDN * HEAD_DIM
CONV_W = 4
GDN_CHUNK = 64
GDN_IN = 4 * W_GDN + 2 * H_GDN
H_RWKV = D_MODEL // 2 // HEAD_DIM
W_RWKV = H_RWKV * HEAD_DIM
RWKV_DECAY_LORA = 64
RWKV_A_LORA = 64
RWKV_GATE_LORA = 128
RWKV_IN = 3 * W_RWKV + RWKV_DECAY_LORA + RWKV_A_LORA + RWKV_GATE_LORA
RWKV_GN_EPS = 64e-5
EV_IN = GDN_IN + RWKV_IN
H_SB = D_MODEL // 2 // HEAD_DIM
W_SB = H_SB * HEAD_DIM
H_DIFF = D_MODEL // 2 // (2 * HEAD_DIM)
W_DIFF = H_DIFF * 2 * HEAD_DIM
OD_IN = 3 * W_SB + 3 * W_DIFF
ROPE_DIM = HEAD_DIM // 4
ROPE_THETA = 500000.0
Q_BLOCK = 128
D_FF = 2816
NORM_EPS = 1e-6

kernel_name = 'hybrid_gdn_rwkv7_stickbreak_diffattn_macaron_step'


def rmsnorm(x, g, eps=NORM_EPS):
    xf = x.astype(jnp.float32)
    y = xf * lax.rsqrt(jnp.mean(xf * xf, axis=-1, keepdims=True) + eps)
    return (y * g.astype(jnp.float32)).astype(x.dtype)


def l2norm(x, eps=1e-6):
    xf = x.astype(jnp.float32)
    return (xf * lax.rsqrt(jnp.sum(xf * xf, axis=-1, keepdims=True) + eps)).astype(x.dtype)


def swiglu(x, w_gate, w_up, w_down):
    return (jax.nn.silu(x @ w_gate) * (x @ w_up)) @ w_down


def split_cols(p, widths):
    offs = np.cumsum(widths)[:-1].tolist()
    return jnp.split(p, offs, axis=-1)


def causal_conv(u, buf, w):
    L = u.shape[1]
    full = jnp.concatenate([buf.astype(u.dtype), u], axis=1)
    y = sum(full[:, j:j + L] * w[j] for j in range(CONV_W))
    return jax.nn.silu(y), full[:, -(CONV_W - 1):]


def rope_partial(x, pos):
    half = ROPE_DIM // 2
    inv_freq = ROPE_THETA ** (-jnp.arange(half, dtype=jnp.float32) * 2.0 / ROPE_DIM)
    ang = pos.astype(jnp.float32)[:, None] * inv_freq[None, :]
    ang = ang.reshape((ang.shape[0],) + (1,) * (x.ndim - 3) + (half,))
    cos, sin = jnp.cos(ang), jnp.sin(ang)
    xr = x[..., :ROPE_DIM].astype(jnp.float32)
    x1, x2 = xr[..., :half], xr[..., half:]
    rot = jnp.concatenate([x1 * cos - x2 * sin, x2 * cos + x1 * sin], axis=-1).astype(x.dtype)
    return jnp.concatenate([rot, x[..., ROPE_DIM:]], axis=-1)


def gather_pages(cache, page_table):
    g = cache[page_table]
    return g.reshape((page_table.shape[0], -1) + cache.shape[2:])


def sweep_query_blocks(fn, q, qpos):
    B, Lq = q.shape[:2]
    if Lq <= Q_BLOCK or Lq % Q_BLOCK:
        return fn(q, qpos)
    nb = Lq // Q_BLOCK
    qb = jnp.moveaxis(q.reshape((B, nb, Q_BLOCK) + q.shape[2:]), 1, 0)
    pb = qpos.reshape(nb, Q_BLOCK)
    ob = lax.map(lambda a: fn(a[0], a[1]), (qb, pb))
    return jnp.moveaxis(ob, 0, 1).reshape((B, Lq) + ob.shape[3:])


def gated_delta_rule(q, k, v, g, beta, s0):
    B, L, H, DK = q.shape
    DV = v.shape[-1]
    C = min(GDN_CHUNK, L)
    pad = (-L) % C
    N = (L + pad) // C

    def prep(t):
        t = t.astype(jnp.float32)
        t = jnp.pad(t, [(0, 0), (0, pad)] + [(0, 0)] * (t.ndim - 2))
        return jnp.moveaxis(t.reshape((B, N, C) + t.shape[2:]), 3, 1)

    qc = prep(q) * DK ** -0.5
    kc, vc, gc, bc = prep(k), prep(v), prep(g), prep(beta)
    gcum = jnp.cumsum(gc, axis=-1)
    causal = jnp.tril(jnp.ones((C, C), bool))
    strict = jnp.tril(jnp.ones((C, C), bool), -1)
    gdiff = gcum[..., :, None] - gcum[..., None, :]
    decay = jnp.where(causal, jnp.exp(jnp.where(causal, gdiff, 0.0)), 0.0)
    kbeta = kc * bc[..., None]
    a_mat = jnp.where(strict, jnp.einsum('bhncd,bhnsd->bhncs', kbeta, kc) * decay, 0.0)
    rhs = jnp.concatenate([vc * bc[..., None], kbeta * jnp.exp(gcum)[..., None]], axis=-1)
    sol = lax.linalg.triangular_solve(a_mat + jnp.eye(C, dtype=jnp.float32), rhs,
                                      left_side=True, lower=True, unit_diagonal=True)
    u, w = sol[..., :DV], sol[..., DV:]
    qk = jnp.einsum('bhncd,bhnsd->bhncs', qc, kc) * decay
    q_dec = qc * jnp.exp(gcum)[..., None]
    k_dec = kc * jnp.exp(gcum[..., -1:] - gcum)[..., None]
    g_last = jnp.exp(gcum[..., -1])

    def step(S, xs):
        u_n, w_n, qk_n, qd_n, kd_n, gl_n = xs
        v_new = u_n - jnp.einsum('bhcd,bhde->bhce', w_n, S)
        o_n = jnp.einsum('bhcd,bhde->bhce', qd_n, S) + jnp.einsum('bhcs,bhse->bhce', qk_n, v_new)
        S = S * gl_n[..., None, None] + jnp.einsum('bhcd,bhce->bhde', kd_n, v_new)
        return S, o_n

    xs = tuple(jnp.moveaxis(t, 2, 0) for t in (u, w, qk, q_dec, k_dec, g_last))
    S, o = lax.scan(step, s0.astype(jnp.float32), xs)
    o = jnp.moveaxis(o, 0, 2).reshape(B, H, N * C, DV)[:, :, :L]
    return jnp.moveaxis(o, 1, 2), S


def gdn_mixer(p, conv_buf, s0, conv_w, a_log, dt_bias, out_g):
    B, L, _ = p.shape
    qkv, z, a, b = split_cols(p, [3 * W_GDN, W_GDN, H_GDN, H_GDN])
    qkv, conv_new = causal_conv(qkv, conv_buf, conv_w)
    q, k, v = (t.reshape(B, L, H_GDN, HEAD_DIM) for t in split_cols(qkv, [W_GDN] * 3))
    g = -jnp.exp(a_log.astype(jnp.float32)) * jax.nn.softplus(a.astype(jnp.float32) + dt_bias.astype(jnp.float32))
    beta = jax.nn.sigmoid(b.astype(jnp.float32))
    o, s_new = gated_delta_rule(l2norm(q), l2norm(k), v, g, beta, s0)
    o = rmsnorm(o.astype(p.dtype), out_g) * jax.nn.silu(z.reshape(B, L, H_GDN, HEAD_DIM))
    return o.reshape(B, L, W_GDN), conv_new, s_new.astype(s0.dtype)


def rwkv7_scan(r, log_w, k, v, kk, a, s0):
    def step(S, xs):
        r_t, lw_t, k_t, v_t, kk_t, a_t = xs
        s_kk = jnp.einsum('bhvk,bhk->bhv', S, kk_t)
        S = (S * jnp.exp(lw_t)[:, :, None, :]
             - s_kk[..., None] * (kk_t * a_t)[:, :, None, :]
             + v_t[..., None] * k_t[:, :, None, :])
        return S, jnp.einsum('bhvk,bhk->bhv', S, r_t)

    xs = tuple(jnp.moveaxis(t.astype(jnp.float32), 1, 0) for t in (r, log_w, k, v, kk, a))
    S, o = lax.scan(step, s0.astype(jnp.float32), xs)
    return jnp.moveaxis(o, 0, 1), S


def rwkv7_mixer(p, shift_prev, s0, mu, w0, w2, a0, a2, g2, k_k, k_a, r_k, ln_w, ln_b):
    B, L, _ = p.shape
    shifted = jnp.concatenate([shift_prev.astype(p.dtype), p[:, :-1]], axis=1)
    xs = p + (shifted - p) * mu
    r, k, v, w_lo, a_lo, g_lo = split_cols(xs, [W_RWKV] * 3 + [RWKV_DECAY_LORA, RWKV_A_LORA, RWKV_GATE_LORA])
    w_log = -jax.nn.softplus(-(w0 + jnp.tanh(w_lo) @ w2)) - 0.5
    log_decay = -jnp.exp(w_log.astype(jnp.float32))
    a = jax.nn.sigmoid(a0 + a_lo @ a2)
    gate = jax.nn.sigmoid(g_lo) @ g2

    def heads(t):
        return t.reshape(B, L, H_RWKV, HEAD_DIM)

    kk = l2norm(heads(k * k_k))
    k = k * (1.0 + (a - 1.0) * k_a)
    r_h, k_h, v_h = heads(r), heads(k), heads(v)
    o, s_new = rwkv7_scan(r_h, heads(log_decay), k_h, v_h, kk, heads(a), s0)
    mean = jnp.mean(o, axis=-1, keepdims=True)
    var = jnp.mean(jnp.square(o - mean), axis=-1, keepdims=True)
    o = ((o - mean) * lax.rsqrt(var + RWKV_GN_EPS) * ln_w.reshape(H_RWKV, HEAD_DIM).astype(jnp.float32)
         + ln_b.reshape(H_RWKV, HEAD_DIM).astype(jnp.float32))
    bonus = jnp.sum((r_h * k_h * r_k).astype(jnp.float32), axis=-1, keepdims=True) * v_h.astype(jnp.float32)
    o = (o + bonus).astype(p.dtype).reshape(B, L, W_RWKV) * gate
    return o, p[:, -1:], s_new.astype(s0.dtype)


def even_mix(hn, conv_buf, s_gdn, shift_prev, s_rwkv, w_in, w_out, conv_w, a_log, dt_bias, out_g,
             mu, w0, w2, a0, a2, g2, k_k, k_a, r_k, ln_w, ln_b):
    p = hn @ w_in
    o_a, conv_new, s_gdn_new = gdn_mixer(p[..., :GDN_IN], conv_buf, s_gdn, conv_w, a_log, dt_bias, out_g)
    o_b, shift_new, s_rwkv_new = rwkv7_mixer(p[..., GDN_IN:], shift_prev, s_rwkv, mu, w0, w2, a0, a2, g2,
                                             k_k, k_a, r_k, ln_w, ln_b)
    out = jnp.concatenate([o_a, o_b], axis=-1) @ w_out
    return out, conv_new, s_gdn_new, shift_new, s_rwkv_new


def stick_breaking_attend(q, qpos, k, v, kpos):
    z = jnp.einsum('bqhd,bkhd->bhqk', q, k).astype(jnp.float32) * HEAD_DIM ** -0.5
    mask = kpos[None, :] < qpos[:, None]
    log_stay = jnp.where(mask, jax.nn.log_sigmoid(-z), 0.0)
    between = lax.cumsum(log_stay, axis=3, reverse=True) - log_stay
    weight = jnp.where(mask, jnp.exp(jax.nn.log_sigmoid(z) + between), 0.0)
    return jnp.einsum('bhqk,bkhd->bqhd', weight.astype(v.dtype), v)


def diff_attend(q, qpos, k, v, kpos, lam):
    s = jnp.einsum('bqhmd,bkhmd->bhmqk', q, k).astype(jnp.float32) * HEAD_DIM ** -0.5
    mask = kpos[None, :] <= qpos[:, None]
    prob = jax.nn.softmax(jnp.where(mask, s, -jnp.inf), axis=-1)
    w = prob[:, :, 0] - lam * prob[:, :, 1]
    return jnp.einsum('bhqk,bkhe->bqhe', w.astype(v.dtype), v)


def odd_mix(hn, pos, past, past_pos, w_in, w_out, lam_vecs, sub_norm, lam_init):
    B, L, _ = hn.shape
    p = hn @ w_in
    q_sb, k_sb, v_sb, q_d, k_d, v_d = split_cols(p, [W_SB] * 3 + [W_DIFF] * 2 + [W_DIFF])
    q_sb, k_sb, v_sb = (t.reshape(B, L, H_SB, HEAD_DIM) for t in (q_sb, k_sb, v_sb))
    q_d = rope_partial(q_d.reshape(B, L, H_DIFF, 2, HEAD_DIM), pos)
    k_d = rope_partial(k_d.reshape(B, L, H_DIFF, 2, HEAD_DIM), pos)
    v_d = v_d.reshape(B, L, H_DIFF, 2 * HEAD_DIM)
    if past is None:
        K_sb, V_sb, K_d, V_d = k_sb, v_sb, k_d, v_d
        kpos = pos
    else:
        K_sb, V_sb, K_d, V_d = (jnp.concatenate([pa, nw], axis=1)
                                for pa, nw in zip(past, (k_sb, v_sb, k_d, v_d)))
        kpos = jnp.concatenate([past_pos, pos])
    lv = lam_vecs.astype(jnp.float32)
    lam = jnp.exp(jnp.sum(lv[0] * lv[1])) - jnp.exp(jnp.sum(lv[2] * lv[3])) + lam_init
    o_sb = sweep_query_blocks(lambda qb, pb: stick_breaking_attend(qb, pb, K_sb, V_sb, kpos), q_sb, pos)
    o_d = sweep_query_blocks(lambda qb, pb: diff_attend(qb, pb, K_d, V_d, kpos, lam), q_d, pos)
    o_d = rmsnorm(o_d, sub_norm) * (1.0 - lam_init)
    out = jnp.concatenate([o_sb.reshape(B, L, W_SB), o_d.reshape(B, L, W_DIFF)], axis=-1) @ w_out
    return out, k_sb, v_sb, k_d, v_d


def setup_inputs(seed: int = 0) -> dict:
    key = jax.random.key(seed)
    ks = iter(jax.random.split(key, 64))
    f32 = jnp.float32

    def nrm(shape, scale):
        return jax.random.normal(next(ks), shape, f32) * scale

    def gain(shape):
        return 1.0 + nrm(shape, 0.01)

    n_pages = PAST_LEN // PAGE_SIZE
    n_phys = (DEC_BATCH * n_pages * 5) // 4
    page_table = jax.random.permutation(next(ks), n_phys)[:DEC_BATCH * n_pages]
    page_table = page_table.reshape(DEC_BATCH, n_pages).astype(jnp.int32)
    dt = jnp.exp(jax.random.uniform(next(ks), (N_EVEN, H_GDN), f32, minval=math.log(1e-3), maxval=math.log(1e-1)))
    return {
        'x_prompt': nrm((BATCH, SEQ, D_MODEL), 1.0),
        'x_sample': nrm((DEC_BATCH, DEC_SEQ, D_MODEL), 1.0),
        'cache_k_sb': nrm((N_ODD, n_phys, PAGE_SIZE, H_SB, HEAD_DIM), 1.0),
        'cache_v_sb': nrm((N_ODD, n_phys, PAGE_SIZE, H_SB, HEAD_DIM), 1.0),
        'cache_k_diff': nrm((N_ODD, n_phys, PAGE_SIZE, H_DIFF, 2, HEAD_DIM), 1.0),
        'cache_v_diff': nrm((N_ODD, n_phys, PAGE_SIZE, H_DIFF, 2 * HEAD_DIM), 1.0),
        'page_table': page_table,
        'state_gdn': nrm((N_EVEN, DEC_BATCH, H_GDN, HEAD_DIM, HEAD_DIM), 0.5),
        'state_gdn_conv': nrm((N_EVEN, DEC_BATCH, CONV_W - 1, 3 * W_GDN), 1.0),
        'state_rwkv': nrm((N_EVEN, DEC_BATCH, H_RWKV, HEAD_DIM, HEAD_DIM), 0.5),
        'state_rwkv_shift': nrm((N_EVEN, DEC_BATCH, 1, RWKV_IN), 1.0),
        'ffn_norm': gain((DEPTH, 2, D_MODEL)),
        'ffn_w_gate': nrm((DEPTH, 2, D_MODEL, D_FF), D_MODEL ** -0.5),
        'ffn_w_up': nrm((DEPTH, 2, D_MODEL, D_FF), D_MODEL ** -0.5),
        'ffn_w_down': nrm((DEPTH, 2, D_FF, D_MODEL), D_FF ** -0.5),
        'mix_norm': gain((DEPTH, D_MODEL)),
        'final_norm': gain((D_MODEL,)),
        'ev_w_in': nrm((N_EVEN, D_MODEL, EV_IN), D_MODEL ** -0.5),
        'ev_w_out': nrm((N_EVEN, W_GDN + W_RWKV, D_MODEL), (W_GDN + W_RWKV) ** -0.5),
        'gdn_conv_w': nrm((N_EVEN, CONV_W, 3 * W_GDN), CONV_W ** -0.5),
        'gdn_a_log': jnp.log(jax.random.uniform(next(ks), (N_EVEN, H_GDN), f32, minval=1.0, maxval=16.0)),
        'gdn_dt_bias': dt + jnp.log(-jnp.expm1(-dt)),
        'gdn_out_norm': gain((N_EVEN, HEAD_DIM)),
        'rwkv_mu': jax.random.uniform(next(ks), (N_EVEN, RWKV_IN), f32),
        'rwkv_w0': jax.random.uniform(next(ks), (N_EVEN, W_RWKV), f32, minval=-6.0, maxval=-1.0),
        'rwkv_w2': nrm((N_EVEN, RWKV_DECAY_LORA, W_RWKV), RWKV_DECAY_LORA ** -0.5),
        'rwkv_a0': nrm((N_EVEN, W_RWKV), 0.1),
        'rwkv_a2': nrm((N_EVEN, RWKV_A_LORA, W_RWKV), RWKV_A_LORA ** -0.5),
        'rwkv_g2': nrm((N_EVEN, RWKV_GATE_LORA, W_RWKV), RWKV_GATE_LORA ** -0.5),
        'rwkv_k_k': 0.85 + nrm((N_EVEN, W_RWKV), 0.02),
        'rwkv_k_a': 1.0 + nrm((N_EVEN, W_RWKV), 0.02),
        'rwkv_r_k': nrm((N_EVEN, H_RWKV, HEAD_DIM), 0.1),
        'rwkv_ln_w': gain((N_EVEN, W_RWKV)),
        'rwkv_ln_b': nrm((N_EVEN, W_RWKV), 0.01),
        'od_w_in': nrm((N_ODD, D_MODEL, OD_IN), D_MODEL ** -0.5),
        'od_w_out': nrm((N_ODD, W_SB + W_DIFF, D_MODEL), (W_SB + W_DIFF) ** -0.5),
        'diff_lambda': nrm((N_ODD, 4, HEAD_DIM), 0.1),
        'diff_norm': gain((N_ODD, 2 * HEAD_DIM)),
    }


def reference(x_prompt, x_sample, cache_k_sb, cache_v_sb, cache_k_diff, cache_v_diff, page_table,
              state_gdn, state_gdn_conv, state_rwkv, state_rwkv_shift,
              ffn_norm, ffn_w_gate, ffn_w_up, ffn_w_down, mix_norm, final_norm,
              ev_w_in, ev_w_out, gdn_conv_w, gdn_a_log, gdn_dt_bias, gdn_out_norm,
              rwkv_mu, rwkv_w0, rwkv_w2, rwkv_a0, rwkv_a2, rwkv_g2, rwkv_k_k, rwkv_k_a, rwkv_r_k,
              rwkv_ln_w, rwkv_ln_b, od_w_in, od_w_out, diff_lambda, diff_norm):
    dtp, dts = x_prompt.dtype, x_sample.dtype
    bp, lp, _ = x_prompt.shape
    bs, ls, _ = x_sample.shape
    past_len = page_table.shape[1] * cache_k_sb.shape[2]
    pos_p = jnp.arange(lp)
    pos_s = past_len + jnp.arange(ls)
    past_pos = jnp.arange(past_len)
    names = ('k_sb', 'v_sb', 'k_diff', 'v_diff', 'gdn_conv', 'gdn', 'rwkv_shift', 'rwkv')
    new_p = {n: [] for n in names}
    new_s = {n: [] for n in names}
    hp, hs = x_prompt, x_sample
    for l in range(DEPTH):
        i = l // 2
        ffa = (ffn_w_gate[l, 0], ffn_w_up[l, 0], ffn_w_down[l, 0])
        ffb = (ffn_w_gate[l, 1], ffn_w_up[l, 1], ffn_w_down[l, 1])
        hp = hp + 0.5 * swiglu(rmsnorm(hp, ffn_norm[l, 0]), *ffa)
        hs = hs + 0.5 * swiglu(rmsnorm(hs, ffn_norm[l, 0]), *ffa)
        hpn, hsn = rmsnorm(hp, mix_norm[l]), rmsnorm(hs, mix_norm[l])
        if l % 2 == 0:
            ev = (ev_w_in[i], ev_w_out[i], gdn_conv_w[i], gdn_a_log[i], gdn_dt_bias[i], gdn_out_norm[i],
                  rwkv_mu[i], rwkv_w0[i], rwkv_w2[i], rwkv_a0[i], rwkv_a2[i], rwkv_g2[i],
                  rwkv_k_k[i], rwkv_k_a[i], rwkv_r_k[i], rwkv_ln_w[i], rwkv_ln_b[i])
            mp, cp, gp, shp, rp = even_mix(hpn,
                                           jnp.zeros((bp, CONV_W - 1, 3 * W_GDN), dtp),
                                           jnp.zeros((bp, H_GDN, HEAD_DIM, HEAD_DIM), dtp),
                                           jnp.zeros((bp, 1, RWKV_IN), dtp),
                                           jnp.zeros((bp, H_RWKV, HEAD_DIM, HEAD_DIM), dtp), *ev)
            ms, cs, gs, shs, rs = even_mix(hsn, state_gdn_conv[i], state_gdn[i],
                                           state_rwkv_shift[i], state_rwkv[i], *ev)
            for n, a_p, a_s in (('gdn_conv', cp, cs), ('gdn', gp, gs), ('rwkv_shift', shp, shs), ('rwkv', rp, rs)):
                new_p[n].append(a_p)
                new_s[n].append(a_s)
        else:
            od = (od_w_in[i], od_w_out[i], diff_lambda[i], diff_norm[i], 0.8 - 0.6 * math.exp(-0.3 * l))
            mp, ksp, vsp, kdp, vdp = odd_mix(hpn, pos_p, None, None, *od)
            past = tuple(gather_pages(c[i], page_table).astype(dts)
                         for c in (cache_k_sb, cache_v_sb, cache_k_diff, cache_v_diff))
            ms, kss, vss, kds, vds = odd_mix(hsn, pos_s, past, past_pos, *od)
            for n, a_p, a_s in (('k_sb', ksp, kss), ('v_sb', vsp, vss), ('k_diff', kdp, kds), ('v_diff', vdp, vds)):
                new_p[n].append(a_p)
                new_s[n].append(a_s)
        hp = hp + mp
        hs = hs + ms
        hp = hp + 0.5 * swiglu(rmsnorm(hp, ffn_norm[l, 1]), *ffb)
        hs = hs + 0.5 * swiglu(rmsnorm(hs, ffn_norm[l, 1]), *ffb)
    y_prompt = rmsnorm(hp, final_norm)
    y_sample = rmsnorm(hs, final_norm)
    return (y_prompt, y_sample,
            jnp.stack(new_p['k_sb']), jnp.stack(new_s['k_sb']),
            jnp.stack(new_p['v_sb']), jnp.stack(new_s['v_sb']),
            jnp.stack(new_p['k_diff']), jnp.stack(new_s['k_diff']),
            jnp.stack(new_p['v_diff']), jnp.stack(new_s['v_diff']),
            jnp.stack(new_p['gdn']), jnp.stack(new_s['gdn']),
            jnp.stack(new_p['gdn_conv']), jnp.stack(new_s['gdn_conv']),
            jnp.stack(new_p['rwkv']), jnp.stack(new_s['rwkv']),
            jnp.stack(new_p['rwkv_shift']), jnp.stack(new_s['rwkv_shift']))
```

```python
import functools
import math

import jax
import jax.numpy as jnp
import numpy as np
from jax import lax
from jax.experimental import pallas as pl
from jax.experimental.pallas import tpu as pltpu

F32 = jnp.float32
BF16 = jnp.bfloat16

HEAD_DIM = 64
HEAD_SHIFT = 6
NORM_EPS = 1e-6
L2_EPS = 1e-6
RWKV_GN_EPS = 64e-5
ROPE_DIM = HEAD_DIM // 4
ROPE_THETA = 500000.0
CONV_W = 4
ATTN_SCALE = HEAD_DIM ** -0.5
NEG_BIG = -1e30

V7X_VMEM_LIMIT_BYTES = 56 * 1024 * 1024
SUBLANES = 8
LANES = 128


def _cparams(*sem):
    return pltpu.CompilerParams(dimension_semantics=sem, vmem_limit_bytes=V7X_VMEM_LIMIT_BYTES)


def _sigmoid(x):
    return 1.0 / (1.0 + jnp.exp(-x))


def _softplus(x):
    return jnp.maximum(x, 0.0) + jnp.log(1.0 + jnp.exp(-jnp.abs(x)))


def _rms(x, eps=NORM_EPS):
    return x * lax.rsqrt(jnp.mean(x * x, axis=-1, keepdims=True) + eps)


def _dot(a, b):
    return jnp.dot(a.astype(BF16), b.astype(BF16), preferred_element_type=F32)


def _dot_nt(a, b):
    return lax.dot_general(a.astype(BF16), b.astype(BF16), (((1,), (1,)), ((), ())),
                           preferred_element_type=F32)


def _dot_tn(a, b):
    return lax.dot_general(a.astype(BF16), b.astype(BF16), (((0,), (0,)), ((), ())),
                           preferred_element_type=F32)


def _dot_exact(a, b):
    return jnp.dot(a, b, precision=lax.Precision.HIGHEST, preferred_element_type=F32)


def _dot_split(x, m_bf16):
    hi = x.astype(BF16)
    lo = (x - hi.astype(F32)).astype(BF16)
    return (jnp.dot(hi, m_bf16, preferred_element_type=F32)
            + jnp.dot(lo, m_bf16, preferred_element_type=F32))


def _iota2(shape, dim):
    return lax.broadcasted_iota(jnp.int32, shape, dim)


def _inv_i_minus(n_mat, size):
    eye = (_iota2((size, size), 0) == _iota2((size, size), 1)).astype(F32)
    t = eye + n_mat
    p = n_mat
    k = 1
    while 2 * k < size:
        p = _dot(p, p)
        t = t + _dot(t, p)
        k *= 2
    return t


def _ffn_body(h_ref, g_ref, wg_ref, wu_ref, wd_ref, fg_ref, o_ref, xn_ref, acc_ref, *, apply_final):
    j = pl.program_id(1)

    @pl.when(j == 0)
    def _():
        xn_ref[...] = (_rms(h_ref[...]) * g_ref[...]).astype(BF16)
        acc_ref[...] = jnp.zeros_like(acc_ref)

    xn = xn_ref[...]
    gate = jnp.dot(xn, wg_ref[...], preferred_element_type=F32)
    up = jnp.dot(xn, wu_ref[...], preferred_element_type=F32)
    act = (gate * _sigmoid(gate)) * up
    acc_ref[...] += jnp.dot(act.astype(BF16), wd_ref[...], preferred_element_type=F32)

    @pl.when(j == pl.num_programs(1) - 1)
    def _():
        out = h_ref[...] + 0.5 * acc_ref[...]
        if apply_final:
            out = _rms(out) * fg_ref[...]
        o_ref[...] = out


def _ffn_half_step(h, norm_g, wg, wu, wd, final_g=None):
    t, d = h.shape
    f = wg.shape[1]
    tm = _pick_tile(t, (512, 256, 128, 64, 32, 16, 8))
    tf = _pick_tile(f, (1408, 1024, 512, 256, 128))
    apply_final = final_g is not None
    fg = final_g if apply_final else norm_g
    return pl.pallas_call(
        functools.partial(_ffn_body, apply_final=apply_final),
        grid=(t // tm, f // tf),
        in_specs=[
            pl.BlockSpec((tm, d), lambda i, j: (i, 0)),
            pl.BlockSpec((1, d), lambda i, j: (0, 0)),
            pl.BlockSpec((d, tf), lambda i, j: (0, j)),
            pl.BlockSpec((d, tf), lambda i, j: (0, j)),
            pl.BlockSpec((tf, d), lambda i, j: (j, 0)),
            pl.BlockSpec((1, d), lambda i, j: (0, 0)),
        ],
        out_specs=pl.BlockSpec((tm, d), lambda i, j: (i, 0)),
        out_shape=jax.ShapeDtypeStruct((t, d), F32),
        scratch_shapes=[pltpu.VMEM((tm, d), BF16), pltpu.VMEM((tm, d), F32)],
        compiler_params=_cparams("parallel", "arbitrary"),
        name="ffn_half_step",
    )(h, norm_g.reshape(1, d), wg, wu, wd, fg.reshape(1, d))


def _pick_tile(n, candidates):
    for c in candidates:
        if n % c == 0:
            return c
    return n


def _proj_even_body(h_ref, g_ref, w_ref, *o_refs, splits):
    xn = (_rms(h_ref[...]) * g_ref[...]).astype(BF16)
    p = jnp.dot(xn, w_ref[...], preferred_element_type=F32)
    off = 0
    for o_ref, width in zip(o_refs, splits):
        o_ref[...] = p[:, off:off + width]
        off += width


def _proj_even(h, norm_g, w, splits):
    t, d = h.shape
    n = w.shape[1]
    tm = _pick_tile(t, (256, 128, 64, 32, 16, 8))
    return pl.pallas_call(
        functools.partial(_proj_even_body, splits=splits),
        grid=(t // tm,),
        in_specs=[
            pl.BlockSpec((tm, d), lambda i: (i, 0)),
            pl.BlockSpec((1, d), lambda i: (0, 0)),
            pl.BlockSpec((d, n), lambda i: (0, 0)),
        ],
        out_specs=[pl.BlockSpec((tm, s), lambda i: (i, 0)) for s in splits],
        out_shape=[jax.ShapeDtypeStruct((t, s), F32) for s in splits],
        compiler_params=_cparams("parallel"),
        name="proj_even",
    )(h, norm_g.reshape(1, d), w)


def _proj_odd_body(h_ref, g_ref, w_ref, pos_ref, freq_ref, o_ref, *, rope_lo, rope_hi):
    xn = (_rms(h_ref[...]) * g_ref[...]).astype(BF16)
    p = jnp.dot(xn, w_ref[...], preferred_element_type=F32)
    o_ref[...] = p
    ang = pos_ref[...] * freq_ref[...]
    cos, sin = jnp.cos(ang), jnp.sin(ang)
    lane = _iota2(ang.shape, 1) & (HEAD_DIM - 1)
    half = ROPE_DIM // 2
    sin_lo = jnp.where(lane < half, -sin, 0.0)
    sin_hi = jnp.where((lane >= half) & (lane < ROPE_DIM), sin, 0.0)
    for c0 in range(rope_lo, rope_hi, LANES):
        x = p[:, c0:c0 + LANES]
        up = pltpu.roll(x, LANES - half, axis=1)
        down = pltpu.roll(x, half, axis=1)
        o_ref[:, c0:c0 + LANES] = x * cos + up * sin_lo + down * sin_hi


def _proj_odd(h, norm_g, w, pos, freq, rope_lo, rope_hi):
    t, d = h.shape
    n = w.shape[1]
    tm = _pick_tile(t, (256, 128, 64, 32, 16, 8))
    return pl.pallas_call(
        functools.partial(_proj_odd_body, rope_lo=rope_lo, rope_hi=rope_hi),
        grid=(t // tm,),
        in_specs=[
            pl.BlockSpec((tm, d), lambda i: (i, 0)),
            pl.BlockSpec((1, d), lambda i: (0, 0)),
            pl.BlockSpec((d, n), lambda i: (0, 0)),
            pl.BlockSpec((tm, 1), lambda i: (i, 0)),
            pl.BlockSpec((1, LANES), lambda i: (0, 0)),
        ],
        out_specs=pl.BlockSpec((tm, n), lambda i: (i, 0)),
        out_shape=jax.ShapeDtypeStruct((t, n), F32),
        compiler_params=_cparams("parallel"),
        name="proj_odd",
    )(h, norm_g.reshape(1, d), w, pos, freq)


def _outproj_body(h_ref, xa_ref, xb_ref, wa_ref, wb_ref, o_ref):
    o_ref[...] = (h_ref[...]
                  + jnp.dot(xa_ref[...].astype(BF16), wa_ref[...], preferred_element_type=F32)
                  + jnp.dot(xb_ref[...].astype(BF16), wb_ref[...], preferred_element_type=F32))


def _outproj(h, xa, xb, wa, wb):
    t, d = h.shape
    ka, kb = xa.shape[1], xb.shape[1]
    tm = _pick_tile(t, (512, 256, 128, 64, 32, 16, 8))
    return pl.pallas_call(
        _outproj_body,
        grid=(t // tm,),
        in_specs=[
            pl.BlockSpec((tm, d), lambda i: (i, 0)),
            pl.BlockSpec((tm, ka), lambda i: (i, 0)),
            pl.BlockSpec((tm, kb), lambda i: (i, 0)),
            pl.BlockSpec((ka, d), lambda i: (0, 0)),
            pl.BlockSpec((kb, d), lambda i: (0, 0)),
        ],
        out_specs=pl.BlockSpec((tm, d), lambda i: (i, 0)),
        out_shape=jax.ShapeDtypeStruct((t, d), F32),
        compiler_params=_cparams("parallel"),
        name="mix_outproj",
    )(h, xa, xb, wa, wb)


def _gdn_body(qkv_ref, z_ref, ab_ref, cst_ref, s0_ref, cw_ref, alog_ref, dtb_ref, og_ref, bd_ref, ex_ref,
              o_ref, cnew_ref, snew_ref,
              xbuf, s_scr, q_s, k_s, v_s, b_s, g_s, *, tb, chunk, valid, n_heads):
    blk = pl.program_id(1)
    w_all = n_heads * HEAD_DIM
    pad = CONV_W - 1

    @pl.when(blk == 0)
    def _():
        xbuf[SUBLANES - pad:SUBLANES, :] = cst_ref[0]
        s_scr[...] = s0_ref[0]

    xbuf[SUBLANES:SUBLANES + tb, :] = qkv_ref[...]
    cw = cw_ref[...]
    y = xbuf[SUBLANES - pad:SUBLANES - pad + tb, :] * cw[0:1]
    for j in range(1, CONV_W):
        y = y + xbuf[SUBLANES - pad + j:SUBLANES - pad + j + tb, :] * cw[j:j + 1]
    qkv = y * _sigmoid(y)
    carry = xbuf[SUBLANES - pad + valid:SUBLANES + valid, :]
    xbuf[SUBLANES - pad:SUBLANES, :] = carry

    @pl.when(blk == pl.num_programs(1) - 1)
    def _():
        cnew_ref[0] = carry

    bd = bd_ref[...]
    q = qkv[:, 0:w_all]
    k = qkv[:, w_all:2 * w_all]
    q_s[...] = q * lax.rsqrt(_dot_split(q * q, bd) + L2_EPS) * ATTN_SCALE
    k_s[...] = k * lax.rsqrt(_dot_split(k * k, bd) + L2_EPS)
    v_s[...] = qkv[:, 2 * w_all:3 * w_all]

    ab = ab_ref[...]
    g = -jnp.exp(alog_ref[...]) * _softplus(ab + dtb_ref[...])
    beta = _sigmoid(ab)
    if valid < tb:
        live = _iota2(ab.shape, 0) < valid
        g = jnp.where(live, g, 0.0)
        beta = jnp.where(live, beta, 0.0)
    g_s[...] = g
    b_s[...] = _dot_exact(beta, ex_ref[...])

    c = chunk
    row = _iota2((c, c), 0)
    col = _iota2((c, c), 1)
    incl = row >= col
    strict = row > col
    tril = incl.astype(F32)
    og = og_ref[...]

    def chunk_step(n, _):
        r0 = n * c if isinstance(n, int) else pl.multiple_of(n * c, c)
        gc = _dot_exact(tril, g_s[pl.ds(r0, c), :])
        gct = gc.T
        for pair in range(n_heads // 2):
            lanes = slice(pair * LANES, (pair + 1) * LANES)
            q2 = q_s[pl.ds(r0, c), lanes]
            k2 = k_s[pl.ds(r0, c), lanes]
            v2 = v_s[pl.ds(r0, c), lanes]
            b2 = b_s[pl.ds(r0, c), lanes]
            z2 = z_ref[pl.ds(r0, c), lanes]
            outs = []
            for sub in range(2):
                h = 2 * pair + sub
                hs = slice(sub * HEAD_DIM, (sub + 1) * HEAD_DIM)
                qh, kh, vh, bh = q2[:, hs], k2[:, hs], v2[:, hs], b2[:, hs]
                gcol = gc[:, h:h + 1]
                grow = gct[h:h + 1, :]
                decay = jnp.where(incl, jnp.exp(jnp.where(incl, gcol - grow, 0.0)), 0.0)
                eg = jnp.exp(gcol)
                glast = gc[c - 1:c, h:h + 1]
                kb = kh * bh
                a_mat = jnp.where(strict, _dot_nt(kb, kh) * decay, 0.0)
                t_inv = _inv_i_minus(-a_mat, c)
                sol = _dot(t_inv, jnp.concatenate([vh * bh, kb * eg], axis=1))
                u, w = sol[:, :HEAD_DIM], sol[:, HEAD_DIM:]
                qk = jnp.where(incl, _dot_nt(qh, kh) * decay, 0.0)
                s_mat = s_scr[h]
                v_new = u - _dot(w, s_mat)
                o = _dot(qh * eg, s_mat) + _dot(qk, v_new)
                s_scr[h] = s_mat * jnp.exp(glast) + _dot_tn(kh * jnp.exp(glast - gcol), v_new)
                outs.append(_rms(o) * og)
            z_gate = z2 * _sigmoid(z2)
            o_ref[pl.ds(r0, c), lanes] = jnp.concatenate(outs, axis=1) * z_gate
        return 0

    n_chunks = tb // c
    if n_chunks == 1:
        chunk_step(0, 0)
    else:
        lax.fori_loop(0, n_chunks, chunk_step, 0)

    @pl.when(blk == pl.num_programs(1) - 1)
    def _():
        snew_ref[0] = s_scr[...]


def _gdn(qkv, z, ab, conv_state, s0, conv_w, a_log, dt_bias, out_g, *, seq_rows, valid, chunk, tb):
    n_seq, n_heads = s0.shape[0], s0.shape[1]
    w_all = n_heads * HEAD_DIM
    n_blk = seq_rows // tb
    lane_h = np.arange(w_all) // HEAD_DIM
    bd = jnp.asarray(lane_h[:, None] == lane_h[None, :], BF16)
    ex = jnp.asarray((np.arange(LANES)[:, None] - n_heads) == lane_h[None, :], F32)
    row128 = lambda x: jnp.zeros((1, LANES), F32).at[0, :n_heads].set(x)
    tok = lambda width: pl.BlockSpec((tb, width), lambda s, b: (s * n_blk + b, 0))
    full = lambda a: pl.BlockSpec(a.shape, lambda s, b: (0,) * a.ndim)
    params = (conv_w, row128(a_log), row128(dt_bias), out_g.reshape(1, HEAD_DIM), bd, ex)
    rows = n_seq * seq_rows
    return pl.pallas_call(
        functools.partial(_gdn_body, tb=tb, chunk=chunk, valid=valid, n_heads=n_heads),
        grid=(n_seq, n_blk),
        in_specs=[tok(3 * w_all), tok(w_all), tok(LANES),
                  pl.BlockSpec((1, CONV_W - 1, 3 * w_all), lambda s, b: (s, 0, 0)),
                  pl.BlockSpec((1, n_heads, HEAD_DIM, HEAD_DIM), lambda s, b: (s, 0, 0, 0))]
        + [full(a) for a in params],
        out_specs=[tok(w_all),
                   pl.BlockSpec((1, CONV_W - 1, 3 * w_all), lambda s, b: (s, 0, 0)),
                   pl.BlockSpec((1, n_heads, HEAD_DIM, HEAD_DIM), lambda s, b: (s, 0, 0, 0))],
        out_shape=[jax.ShapeDtypeStruct((rows, w_all), F32),
                   jax.ShapeDtypeStruct(conv_state.shape, F32),
                   jax.ShapeDtypeStruct(s0.shape, F32)],
        scratch_shapes=[pltpu.VMEM((SUBLANES + tb, 3 * w_all), F32),
                        pltpu.VMEM((n_heads, HEAD_DIM, HEAD_DIM), F32),
                        pltpu.VMEM((tb, w_all), F32), pltpu.VMEM((tb, w_all), F32),
                        pltpu.VMEM((tb, w_all), F32), pltpu.VMEM((tb, w_all), F32),
                        pltpu.VMEM((tb, LANES), F32)],
        compiler_params=_cparams("parallel", "arbitrary"),
        name="gdn_mixer",
    )(qkv, z, ab, conv_state, s0, *params)


def _rwkv_body(p_ref, sh_ref, s0_ref, mu_ref, w0_ref, a0_ref, kk_ref, ka_ref, rk_ref, lnw_ref, lnb_ref,
               w2_ref, a2_ref, g2_ref, bd_ref,
               o_ref, shnew_ref, snew_ref,
               xbuf, s_scr, r_s, k_s, v_s, kk_s, a_s, ld_s, gate_s, bonus_s, *, tb, chunk, valid, n_heads):
    blk = pl.program_id(1)
    w_all = n_heads * HEAD_DIM

    @pl.when(blk == 0)
    def _():
        xbuf[SUBLANES - 1:SUBLANES, :] = sh_ref[0]
        s_scr[...] = s0_ref[0]

    p = p_ref[...]
    xbuf[SUBLANES:SUBLANES + tb, :] = p
    shifted = xbuf[SUBLANES - 1:SUBLANES - 1 + tb, :]
    xs = p + (shifted - p) * mu_ref[...]
    last = xbuf[SUBLANES - 1 + valid:SUBLANES + valid, :]
    xbuf[SUBLANES - 1:SUBLANES, :] = last

    @pl.when(blk == pl.num_programs(1) - 1)
    def _():
        shnew_ref[0] = last

    r = xs[:, 0:w_all]
    k = xs[:, w_all:2 * w_all]
    v = xs[:, 2 * w_all:3 * w_all]
    lo = xs[:, 3 * w_all:3 * w_all + LANES]
    g_lo = xs[:, 3 * w_all + LANES:]
    w_log = -_softplus(-(w0_ref[...] + _dot(jnp.tanh(lo), w2_ref[...]))) - 0.5
    log_decay = -jnp.exp(w_log)
    a = _sigmoid(a0_ref[...] + _dot(lo, a2_ref[...]))
    gate_s[...] = _dot(_sigmoid(g_lo), g2_ref[...])
    bd = bd_ref[...]
    kkx = k * kk_ref[...]
    kk = kkx * lax.rsqrt(_dot_split(kkx * kkx, bd) + L2_EPS)
    k = k * (1.0 + (a - 1.0) * ka_ref[...])
    bonus_s[...] = _dot_split(r * k * rk_ref[...], bd) * v
    if valid < tb:
        live = _iota2(r.shape, 0) < valid
        log_decay = jnp.where(live, log_decay, 0.0)
        kk = jnp.where(live, kk, 0.0)
        v = jnp.where(live, v, 0.0)
    r_s[...] = r
    k_s[...] = k
    v_s[...] = v
    kk_s[...] = kk
    a_s[...] = a
    ld_s[...] = log_decay

    c = chunk
    row = _iota2((c, c), 0)
    col = _iota2((c, c), 1)
    incl = row >= col
    strict = row > col
    tril = incl.astype(F32)

    def chunk_step(n, _):
        r0 = n * c if isinstance(n, int) else pl.multiple_of(n * c, c)
        rows = pl.ds(r0, c)
        for pair in range(n_heads // 2):
            lanes = slice(pair * LANES, (pair + 1) * LANES)
            ld = ld_s[rows, lanes]
            gcum = _dot_exact(tril, ld)
            e_pos = jnp.exp(gcum)
            e_neg = jnp.exp(-gcum)
            kk2 = kk_s[rows, lanes]
            r_t = r_s[rows, lanes] * e_pos
            b_t = kk2 * jnp.exp(gcum - ld)
            a_t = -(kk2 * a_s[rows, lanes]) * e_neg
            k_t = k_s[rows, lanes] * e_neg
            e_last = e_pos[c - 1:c, :]
            a_l = a_t * e_last
            k_l = k_t * e_last
            v2 = v_s[rows, lanes]
            outs = []
            for sub in range(2):
                h = 2 * pair + sub
                hs = slice(sub * HEAD_DIM, (sub + 1) * HEAD_DIM)
                vh = v2[:, hs]
                left = jnp.concatenate([b_t[:, hs], r_t[:, hs]], axis=0)
                right = jnp.concatenate([a_t[:, hs], k_t[:, hs]], axis=0)
                m = _dot_nt(left, right)
                a_ab = jnp.where(strict, m[:c, :c], 0.0)
                a_kb = jnp.where(strict, m[:c, c:], 0.0)
                q_a = jnp.where(incl, m[c:, :c], 0.0)
                q_k = jnp.where(incl, m[c:, c:], 0.0)
                s_vk = s_scr[h]
                ls = _dot_nt(left, s_vk)
                t_inv = _inv_i_minus(a_ab, c)
                u = _dot(t_inv, ls[:c] + _dot(a_kb, vh))
                uv = jnp.concatenate([u, vh], axis=0)
                o = ls[c:] + _dot(jnp.concatenate([q_a, q_k], axis=1), uv)
                s_scr[h] = (s_vk * e_last[:, hs]
                            + _dot_tn(uv, jnp.concatenate([a_l[:, hs], k_l[:, hs]], axis=0)))
                mean = jnp.mean(o, axis=-1, keepdims=True)
                var = jnp.mean(jnp.square(o - mean), axis=-1, keepdims=True)
                outs.append((o - mean) * lax.rsqrt(var + RWKV_GN_EPS))
            on = jnp.concatenate(outs, axis=1) * lnw_ref[:, lanes] + lnb_ref[:, lanes]
            o_ref[rows, lanes] = (on + bonus_s[rows, lanes]) * gate_s[rows, lanes]
        return 0

    n_chunks = tb // c
    if n_chunks == 1:
        chunk_step(0, 0)
    else:
        lax.fori_loop(0, n_chunks, chunk_step, 0)

    @pl.when(blk == pl.num_programs(1) - 1)
    def _():
        snew_ref[0] = s_scr[...]


def _rwkv(p, shift_state, s0, mu, w0, w2, a0, a2, g2, k_k, k_a, r_k, ln_w, ln_b, *, seq_rows, valid, chunk, tb):
    n_seq, n_heads = s0.shape[0], s0.shape[1]
    w_all = n_heads * HEAD_DIM
    width = p.shape[1]
    n_blk = seq_rows // tb
    lane_h = np.arange(w_all) // HEAD_DIM
    bd = jnp.asarray(lane_h[:, None] == lane_h[None, :], BF16)
    lora = w2.shape[0]
    w2p = jnp.zeros((LANES, w_all), BF16).at[:lora].set(w2.astype(BF16))
    a2p = jnp.zeros((LANES, w_all), BF16).at[lora:lora + a2.shape[0]].set(a2.astype(BF16))
    r1 = lambda x: x.reshape(1, -1)
    params = (r1(mu), r1(w0), r1(a0), r1(k_k), r1(k_a), r1(r_k), r1(ln_w), r1(ln_b),
              w2p, a2p, g2.astype(BF16), bd)
    tok = lambda wd: pl.BlockSpec((tb, wd), lambda s, b: (s * n_blk + b, 0))
    full = lambda arr: pl.BlockSpec(arr.shape, lambda s, b: (0,) * arr.ndim)
    rows = n_seq * seq_rows
    return pl.pallas_call(
        functools.partial(_rwkv_body, tb=tb, chunk=chunk, valid=valid, n_heads=n_heads),
        grid=(n_seq, n_blk),
        in_specs=[tok(width),
                  pl.BlockSpec((1, 1, width), lambda s, b: (s, 0, 0)),
                  pl.BlockSpec((1, n_heads, HEAD_DIM, HEAD_DIM), lambda s, b: (s, 0, 0, 0))]
        + [full(arr) for arr in params],
        out_specs=[tok(w_all),
                   pl.BlockSpec((1, 1, width), lambda s, b: (s, 0, 0)),
                   pl.BlockSpec((1, n_heads, HEAD_DIM, HEAD_DIM), lambda s, b: (s, 0, 0, 0))],
        out_shape=[jax.ShapeDtypeStruct((rows, w_all), F32),
                   jax.ShapeDtypeStruct(shift_state.shape, F32),
                   jax.ShapeDtypeStruct(s0.shape, F32)],
        scratch_shapes=[pltpu.VMEM((SUBLANES + tb, width), F32),
                        pltpu.VMEM((n_heads, HEAD_DIM, HEAD_DIM), F32)]
        + [pltpu.VMEM((tb, w_all), F32) for _ in range(8)],
        compiler_params=_cparams("parallel", "arbitrary"),
        name="rwkv7_mixer",
    )(p, shift_state, s0, *params)


def _sb_prompt_body(q_ref, k_ref, v_ref, u_ref, o_ref, *, tq, tk):
    qi = pl.program_id(2)
    n_kb = (qi + 1) * (tq // tk)
    lane = _iota2((tq, LANES), 1)
    q = q_ref[...] * ATTN_SCALE
    q_heads = [jnp.where(lane < HEAD_DIM, q, 0.0).astype(BF16),
               jnp.where(lane >= HEAD_DIM, q, 0.0).astype(BF16)]
    qpos = qi * tq + _iota2((tq, tk), 0)
    kidx = _iota2((tq, tk), 1)
    u = u_ref[...]

    def body(jj, carry):
        kb = n_kb - 1 - jj
        k0 = pl.multiple_of(kb * tk, tk)
        k = k_ref[pl.ds(k0, tk), :].astype(BF16)
        v = v_ref[pl.ds(k0, tk), :].astype(BF16)
        mask = (k0 + kidx) < qpos
        new = []
        for hh in range(2):
            c, acc = carry[2 * hh], carry[2 * hh + 1]
            z = lax.dot_general(q_heads[hh], k, (((1,), (1,)), ((), ())), preferred_element_type=F32)
            sp = _softplus(z)
            ls = jnp.where(mask, -sp, 0.0)
            between = c + _dot_split(ls, u)
            w = jnp.where(mask, jnp.exp(z - sp + between), 0.0)
            acc = acc + jnp.dot(w.astype(BF16), v, preferred_element_type=F32)
            c = c + jnp.sum(ls, axis=-1, keepdims=True)
            new += [c, acc]
        return tuple(new)

    init = (jnp.zeros((tq, 1), F32), jnp.zeros((tq, LANES), F32)) * 2
    res = lax.fori_loop(0, n_kb, body, init)
    o_ref[...] = jnp.where(lane < HEAD_DIM, res[1], res[3])


def _sb_prompt(p, n_batch, seq, col_q, col_k, col_v, n_heads):
    tq = _pick_tile(seq, (256, 128, 64, 32, 16, 8))
    tk = _pick_tile(tq, (128, 64, 32, 16, 8))
    n_q = seq // tq
    u = jnp.asarray(np.arange(tk)[:, None] > np.arange(tk)[None, :], BF16)
    return pl.pallas_call(
        functools.partial(_sb_prompt_body, tq=tq, tk=tk),
        grid=(n_batch, n_heads // 2, n_q),
        in_specs=[
            pl.BlockSpec((tq, LANES), lambda b, hp, i: (b * n_q + i, col_q + hp)),
            pl.BlockSpec((seq, LANES), lambda b, hp, i: (b, col_k + hp)),
            pl.BlockSpec((seq, LANES), lambda b, hp, i: (b, col_v + hp)),
            pl.BlockSpec((tk, tk), lambda b, hp, i: (0, 0)),
        ],
        out_specs=pl.BlockSpec((tq, LANES), lambda b, hp, i: (b * n_q + i, hp)),
        out_shape=jax.ShapeDtypeStruct((n_batch * seq, n_heads * HEAD_DIM), F32),
        compiler_params=_cparams("parallel", "parallel", "arbitrary"),
        name="sb_attn_prompt",
    )(p, p, p, u)


def _lambda_of(lam_ref, lam_init):
    lv = lam_ref[...]
    return (jnp.exp(jnp.sum(lv[0:1] * lv[1:2], axis=-1, keepdims=True))
            - jnp.exp(jnp.sum(lv[2:3] * lv[3:4], axis=-1, keepdims=True)) + lam_init)


def _diff_prompt_body(q_ref, k_ref, v_ref, lam_ref, sn_ref, o_ref, *, tq, tk, lam_init):
    qi = pl.program_id(2)
    n_kb = (qi + 1) * (tq // tk)
    lane = _iota2((tq, LANES), 1)
    q = q_ref[...] * ATTN_SCALE
    q_maps = [jnp.where(lane < HEAD_DIM, q, 0.0).astype(BF16),
              jnp.where(lane >= HEAD_DIM, q, 0.0).astype(BF16)]
    qpos = qi * tq + _iota2((tq, tk), 0)
    kidx = _iota2((tq, tk), 1)

    def body(kb, carry):
        k0 = pl.multiple_of(kb * tk, tk)
        k = k_ref[pl.ds(k0, tk), :].astype(BF16)
        v = v_ref[pl.ds(k0, tk), :].astype(BF16)
        mask = (k0 + kidx) <= qpos
        new = []
        for mm in range(2):
            m_run, l_run, acc = carry[3 * mm:3 * mm + 3]
            s = lax.dot_general(q_maps[mm], k, (((1,), (1,)), ((), ())), preferred_element_type=F32)
            s = jnp.where(mask, s, NEG_BIG)
            m_new = jnp.maximum(m_run, jnp.max(s, axis=-1, keepdims=True))
            corr = jnp.exp(m_run - m_new)
            pr = jnp.where(mask, jnp.exp(s - m_new), 0.0)
            l_run = l_run * corr + jnp.sum(pr, axis=-1, keepdims=True)
            acc = acc * corr + jnp.dot(pr.astype(BF16), v, preferred_element_type=F32)
            new += [m_new, l_run, acc]
        return tuple(new)

    one = (jnp.full((tq, 1), NEG_BIG, F32), jnp.zeros((tq, 1), F32), jnp.zeros((tq, LANES), F32))
    res = lax.fori_loop(0, n_kb, body, one * 2)
    lam = _lambda_of(lam_ref, lam_init)
    o = res[2] / res[1] - lam * (res[5] / res[4])
    o_ref[...] = _rms(o) * sn_ref[...] * (1.0 - lam_init)


def _diff_prompt(p, n_batch, seq, col_q, col_k, col_v, n_heads, lam_vecs, sub_norm, lam_init):
    tq = _pick_tile(seq, (256, 128, 64, 32, 16, 8))
    tk = _pick_tile(tq, (128, 64, 32, 16, 8))
    n_q = seq // tq
    return pl.pallas_call(
        functools.partial(_diff_prompt_body, tq=tq, tk=tk, lam_init=lam_init),
        grid=(n_batch, n_heads, n_q),
        in_specs=[
            pl.BlockSpec((tq, LANES), lambda b, h, i: (b * n_q + i, col_q + h)),
            pl.BlockSpec((seq, LANES), lambda b, h, i: (b, col_k + h)),
            pl.BlockSpec((seq, LANES), lambda b, h, i: (b, col_v + h)),
            pl.BlockSpec(lam_vecs.shape, lambda b, h, i: (0, 0)),
            pl.BlockSpec((1, LANES), lambda b, h, i: (0, 0)),
        ],
        out_specs=pl.BlockSpec((tq, LANES), lambda b, h, i: (b * n_q + i, h)),
        out_shape=jax.ShapeDtypeStruct((n_batch * seq, n_heads * LANES), F32),
        compiler_params=_cparams("parallel", "parallel", "arbitrary"),
        name="diff_attn_prompt",
    )(p, p, p, lam_vecs, sub_norm.reshape(1, LANES))


def _block_diag_queries(q, n_groups):
    lane_g = _iota2(q.shape, 1) >> HEAD_SHIFT
    return jnp.concatenate([jnp.where(lane_g == g, q, 0.0) for g in range(n_groups)], axis=0)


def _sb_sample_body(pt_ref, q_ref, kn_ref, vn_ref, kt_ref, vt_ref, u_ref, o_ref, qbd_s, c_s, acc_s,
                    *, n_new, n_groups):
    del pt_ref
    j = pl.program_id(1)
    rows = n_groups * SUBLANES
    width = n_groups * HEAD_DIM
    qrow = _iota2((rows, 1), 0) & (SUBLANES - 1)

    @pl.when(j == 0)
    def _():
        qbd = _block_diag_queries(q_ref[...] * ATTN_SCALE, n_groups)
        qbd_s[...] = qbd
        c = jnp.zeros((rows, 1), F32)
        acc = jnp.zeros((rows, width), F32)
        kn = kn_ref[...]
        vn = vn_ref[...]
        for t in range(n_new - 1, -1, -1):
            z = jnp.sum(qbd * kn[t:t + 1, :], axis=-1, keepdims=True)
            vis = t < qrow
            sp = _softplus(z)
            acc = acc + jnp.where(vis, jnp.exp(z - sp + c), 0.0) * vn[t:t + 1, :]
            c = c + jnp.where(vis, -sp, 0.0)
        c_s[...] = c
        acc_s[...] = acc

    kt = kt_ref[...].reshape(width, LANES)
    vt = vt_ref[...].reshape(width, LANES)
    z = _dot(qbd_s[...], kt)
    sp = _softplus(z)
    c = c_s[...]
    between = c - _dot_split(sp, u_ref[...])
    w = jnp.exp(z - sp + between)
    acc_s[...] += _dot_nt(w, vt)
    c_s[...] = c - jnp.sum(sp, axis=-1, keepdims=True)

    @pl.when(j == pl.num_programs(1) - 1)
    def _():
        acc = acc_s[...]
        lane_g = _iota2((SUBLANES, width), 1) >> HEAD_SHIFT
        out = jnp.zeros((SUBLANES, width), F32)
        for g in range(n_groups):
            out = out + jnp.where(lane_g == g, acc[g * SUBLANES:(g + 1) * SUBLANES, :], 0.0)
        o_ref[...] = out


def _sb_sample(page_table, q8, kn8, vn8, cache_kt, cache_vt, layer, n_new):
    n_b, n_pages = page_table.shape
    n_heads = cache_kt.shape[2]
    width = n_heads * HEAD_DIM
    page = cache_kt.shape[-1]
    rows = n_heads * SUBLANES
    u = jnp.asarray(np.arange(page)[:, None] > np.arange(page)[None, :], BF16)
    tok = pl.BlockSpec((None, SUBLANES, width), lambda b, j, pt: (b, 0, 0))
    pg = pl.BlockSpec((None, None, n_heads, HEAD_DIM, page),
                      lambda b, j, pt: (layer, pt[b, n_pages - 1 - j], 0, 0, 0))
    return pl.pallas_call(
        functools.partial(_sb_sample_body, n_new=n_new, n_groups=n_heads),
        grid_spec=pltpu.PrefetchScalarGridSpec(
            num_scalar_prefetch=1,
            grid=(n_b, n_pages),
            in_specs=[tok, tok, tok, pg, pg, pl.BlockSpec((page, page), lambda b, j, pt: (0, 0))],
            out_specs=tok,
            scratch_shapes=[pltpu.VMEM((rows, width), F32), pltpu.VMEM((rows, 1), F32),
                            pltpu.VMEM((rows, width), F32)]),
        out_shape=jax.ShapeDtypeStruct((n_b, SUBLANES, width), F32),
        compiler_params=_cparams("parallel", "arbitrary"),
        name="sb_attn_sample",
    )(page_table, q8, kn8, vn8, cache_kt, cache_vt, u)


def _diff_sample_body(pt_ref, q_ref, kn_ref, vn_ref, kt_ref, v_ref, lam_ref, sn_ref, o_ref,
                      qbd_s, m_s, l_s, acc_s, *, n_new, n_heads, lam_init):
    del pt_ref
    j = pl.program_id(1)
    n_groups = 2 * n_heads
    rows = n_groups * SUBLANES
    hrows = 2 * SUBLANES
    qrow = _iota2((rows, 1), 0) & (SUBLANES - 1)

    @pl.when(j == 0)
    def _():
        qbd = _block_diag_queries(q_ref[...] * ATTN_SCALE, n_groups)
        qbd_s[...] = qbd
        kn = kn_ref[...]
        vn = vn_ref[...]
        m_run = jnp.full((rows, 1), NEG_BIG, F32)
        l_run = jnp.zeros((rows, 1), F32)
        acc = jnp.zeros((rows, LANES), F32)
        for t in range(n_new):
            s = jnp.sum(qbd * kn[t:t + 1, :], axis=-1, keepdims=True)
            vis = t <= qrow
            m_new = jnp.maximum(m_run, jnp.where(vis, s, NEG_BIG))
            corr = jnp.exp(m_run - m_new)
            pr = jnp.where(vis, jnp.exp(s - m_new), 0.0)
            v_rows = jnp.concatenate(
                [jnp.broadcast_to(vn[t:t + 1, h * LANES:(h + 1) * LANES], (hrows, LANES)) for h in range(n_heads)],
                axis=0)
            l_run = l_run * corr + pr
            acc = acc * corr + pr * v_rows
            m_run = m_new
        m_s[...] = m_run
        l_s[...] = l_run
        acc_s[...] = acc

    kt = kt_ref[...].reshape(n_groups * HEAD_DIM, LANES)
    s = _dot(qbd_s[...], kt)
    m_run = m_s[...]
    m_new = jnp.maximum(m_run, jnp.max(s, axis=-1, keepdims=True))
    corr = jnp.exp(m_run - m_new)
    pr = jnp.exp(s - m_new)
    l_s[...] = l_s[...] * corr + jnp.sum(pr, axis=-1, keepdims=True)
    m_s[...] = m_new
    pv = jnp.concatenate(
        [_dot(pr[h * hrows:(h + 1) * hrows, :], v_ref[:, h, :]) for h in range(n_heads)], axis=0)
    acc_s[...] = acc_s[...] * corr + pv

    @pl.when(j == pl.num_programs(1) - 1)
    def _():
        o_all = acc_s[...] / l_s[...]
        lam = _lambda_of(lam_ref, lam_init)
        outs = []
        for h in range(n_heads):
            o = (o_all[h * hrows:h * hrows + SUBLANES, :]
                 - lam * o_all[h * hrows + SUBLANES:(h + 1) * hrows, :])
            outs.append(_rms(o) * sn_ref[...] * (1.0 - lam_init))
        o_ref[...] = jnp.concatenate(outs, axis=1)


def _diff_sample(page_table, q8, kn8, vn8, cache_kt, cache_v, layer, n_new, lam_vecs, sub_norm, lam_init):
    n_b, n_pages = page_table.shape
    n_heads = cache_kt.shape[2]
    page = cache_kt.shape[-1]
    width = n_heads * 2 * HEAD_DIM
    rows = 2 * n_heads * SUBLANES
    tok = pl.BlockSpec((None, SUBLANES, width), lambda b, j, pt: (b, 0, 0))
    return pl.pallas_call(
        functools.partial(_diff_sample_body, n_new=n_new, n_heads=n_heads, lam_init=lam_init),
        grid_spec=pltpu.PrefetchScalarGridSpec(
            num_scalar_prefetch=1,
            grid=(n_b, n_pages),
            in_specs=[tok, tok, tok,
                      pl.BlockSpec((None, None, n_heads, 2, HEAD_DIM, page),
                                   lambda b, j, pt: (layer, pt[b, j], 0, 0, 0, 0)),
                      pl.BlockSpec((None, None, page, n_heads, LANES),
                                   lambda b, j, pt: (layer, pt[b, j], 0, 0, 0)),
                      pl.BlockSpec(lam_vecs.shape, lambda b, j, pt: (0, 0)),
                      pl.BlockSpec((1, LANES), lambda b, j, pt: (0, 0))],
            out_specs=tok,
            scratch_shapes=[pltpu.VMEM((rows, width), F32), pltpu.VMEM((rows, 1), F32),
                            pltpu.VMEM((rows, 1), F32), pltpu.VMEM((rows, LANES), F32)]),
        out_shape=jax.ShapeDtypeStruct((n_b, SUBLANES, width), F32),
        compiler_params=_cparams("parallel", "arbitrary"),
        name="diff_attn_sample",
    )(page_table, q8, kn8, vn8, cache_kt, cache_v, lam_vecs, sub_norm.reshape(1, LANES))


def _pad_rows(x, n_seq, seq, to):
    w = x.shape[-1]
    return jnp.pad(x.reshape(n_seq, seq, w), ((0, 0), (0, to - seq), (0, 0))).reshape(n_seq * to, w)


def _unpad_rows(x, n_seq, seq, to):
    return x.reshape(n_seq, to, x.shape[-1])[:, :seq].reshape(n_seq * seq, x.shape[-1])


def _even_mix(h, norm_g, dims, states, prm):
    bp, lp, bs, ls = dims
    tp = bp * lp
    (w_in, w_out, conv_w, a_log, dt_bias, out_g, mu, w0, w2, a0, a2, g2, k_k, k_a, r_k, ln_w, ln_b) = prm
    conv_s, gdn_s, shift_s, rwkv_s = states
    n_heads = a_log.shape[0]
    w_gdn = n_heads * HEAD_DIM
    gdn_in = 4 * w_gdn + 2 * n_heads
    ab_w = jnp.zeros((w_in.shape[0], LANES), F32).at[:, :2 * n_heads].set(w_in[:, 4 * w_gdn:gdn_in])
    w_cat = jnp.concatenate([w_in[:, :4 * w_gdn], w_in[:, gdn_in:], ab_w], axis=1).astype(BF16)
    rwkv_in = w_in.shape[1] - gdn_in
    qkv, z, p_rwkv, ab = _proj_even(h, norm_g, w_cat, (3 * w_gdn, w_gdn, rwkv_in, LANES))

    chunk_p = min(64, lp)
    tb_p = _pick_tile(lp, (256, 128, 64)) if lp >= 64 else lp
    zeros = lambda shape: jnp.zeros(shape, F32)
    o_a_p, conv_p, gdn_p = _gdn(qkv, z, ab, zeros((bp,) + conv_s.shape[1:]), zeros((bp,) + gdn_s.shape[1:]),
                                conv_w, a_log, dt_bias, out_g, seq_rows=lp, valid=tb_p, chunk=chunk_p, tb=tb_p)
    o_b_p, shift_p, rwkv_p = _rwkv(p_rwkv, zeros((bp,) + shift_s.shape[1:]), zeros((bp,) + rwkv_s.shape[1:]),
                                   mu, w0, w2, a0, a2, g2, k_k, k_a, r_k, ln_w, ln_b,
                                   seq_rows=lp, valid=tb_p, chunk=chunk_p, tb=tb_p)

    ls8 = -(-ls // SUBLANES) * SUBLANES
    pad = lambda x: _pad_rows(x[tp:], bs, ls, ls8)
    o_a_s, conv_n, gdn_n = _gdn(pad(qkv), pad(z), pad(ab), conv_s, gdn_s, conv_w, a_log, dt_bias, out_g,
                                seq_rows=ls8, valid=ls, chunk=ls8, tb=ls8)
    o_b_s, shift_n, rwkv_n = _rwkv(pad(p_rwkv), shift_s, rwkv_s, mu, w0, w2, a0, a2, g2, k_k, k_a, r_k,
                                   ln_w, ln_b, seq_rows=ls8, valid=ls, chunk=ls8, tb=ls8)
    o_a = jnp.concatenate([o_a_p[:tp], _unpad_rows(o_a_s, bs, ls, ls8)], axis=0)
    o_b = jnp.concatenate([o_b_p[:tp], _unpad_rows(o_b_s, bs, ls, ls8)], axis=0)
    w_out_b = w_out.astype(BF16)
    h = _outproj(h, o_a, o_b, w_out_b[:w_gdn], w_out_b[w_gdn:])
    return h, (conv_p, gdn_p, shift_p, rwkv_p), (conv_n, gdn_n, shift_n, rwkv_n)


def _odd_mix(h, norm_g, dims, caches, page_table, layer_idx, pos, freq, prm, lam_init):
    bp, lp, bs, ls = dims
    tp = bp * lp
    w_in, w_out, lam_vecs, sub_norm = prm
    ck_sb, cv_sb, ck_d, cv_d = caches
    h_sb = ck_sb.shape[3]
    h_d = ck_d.shape[3]
    w_sb = h_sb * HEAD_DIM
    w_d = h_d * 2 * HEAD_DIM
    p = _proj_odd(h, norm_g, w_in.astype(BF16), pos, freq, 3 * w_sb, 3 * w_sb + 2 * w_d)
    cb = lambda col: col // LANES
    o_sb_p = _sb_prompt(p, bp, lp, cb(0), cb(w_sb), cb(2 * w_sb), h_sb)
    o_d_p = _diff_prompt(p, bp, lp, cb(3 * w_sb), cb(3 * w_sb + w_d), cb(3 * w_sb + 2 * w_d), h_d,
                         lam_vecs, sub_norm, lam_init)

    ps = p[tp:]
    col = lambda a, b: _pad_rows(ps[:, a:b], bs, ls, SUBLANES).reshape(bs, SUBLANES, b - a)
    o_sb_s = _sb_sample(page_table, col(0, w_sb), col(w_sb, 2 * w_sb), col(2 * w_sb, 3 * w_sb),
                        jnp.transpose(ck_sb, (0, 1, 3, 4, 2)), jnp.transpose(cv_sb, (0, 1, 3, 4, 2)),
                        layer_idx, ls)
    d0 = 3 * w_sb
    o_d_s = _diff_sample(page_table, col(d0, d0 + w_d), col(d0 + w_d, d0 + 2 * w_d), col(d0 + 2 * w_d, d0 + 3 * w_d),
                         jnp.transpose(ck_d, (0, 1, 3, 4, 5, 2)), cv_d, layer_idx, ls, lam_vecs, sub_norm, lam_init)
    unp = lambda x: x[:, :ls].reshape(bs * ls, x.shape[-1])
    o_sb = jnp.concatenate([o_sb_p, unp(o_sb_s)], axis=0)
    o_d = jnp.concatenate([o_d_p, unp(o_d_s)], axis=0)
    w_out_b = w_out.astype(BF16)
    h = _outproj(h, o_sb, o_d, w_out_b[:w_sb], w_out_b[w_sb:])

    def split(rows, b, l):
        k_sb = rows[:, w_sb:2 * w_sb].reshape(b, l, h_sb, HEAD_DIM)
        v_sb = rows[:, 2 * w_sb:3 * w_sb].reshape(b, l, h_sb, HEAD_DIM)
        k_d = rows[:, d0 + w_d:d0 + 2 * w_d].reshape(b, l, h_d, 2, HEAD_DIM)
        v_d = rows[:, d0 + 2 * w_d:d0 + 3 * w_d].reshape(b, l, h_d, 2 * HEAD_DIM)
        return k_sb, v_sb, k_d, v_d

    return h, split(p[:tp], bp, lp), split(ps, bs, ls)


def kernel(x_prompt, x_sample, cache_k_sb, cache_v_sb, cache_k_diff, cache_v_diff, page_table, state_gdn, state_gdn_conv, state_rwkv, state_rwkv_shift, ffn_norm, ffn_w_gate, ffn_w_up, ffn_w_down, mix_norm, final_norm, ev_w_in, ev_w_out, gdn_conv_w, gdn_a_log, gdn_dt_bias, gdn_out_norm, rwkv_mu, rwkv_w0, rwkv_w2, rwkv_a0, rwkv_a2, rwkv_g2, rwkv_k_k, rwkv_k_a, rwkv_r_k, rwkv_ln_w, rwkv_ln_b, od_w_in, od_w_out, diff_lambda, diff_norm):
    bp, lp, d = x_prompt.shape
    bs, ls, _ = x_sample.shape
    depth = ffn_norm.shape[0]
    dims = (bp, lp, bs, ls)
    tp = bp * lp
    past_len = page_table.shape[1] * cache_k_sb.shape[2]

    h = jnp.concatenate([x_prompt.reshape(tp, d), x_sample.reshape(bs * ls, d)], axis=0)
    pos = jnp.concatenate([jnp.tile(jnp.arange(lp), bp), jnp.tile(past_len + jnp.arange(ls), bs)])
    pos = pos.astype(F32).reshape(-1, 1)
    half = ROPE_DIM // 2
    inv_freq = ROPE_THETA ** (-jnp.arange(half, dtype=F32) * 2.0 / ROPE_DIM)
    lane = np.arange(LANES) % HEAD_DIM
    freq = jnp.where(lane < ROPE_DIM, inv_freq[lane % half], 0.0).astype(F32).reshape(1, LANES)

    wg, wu, wd = (w.astype(BF16) for w in (ffn_w_gate, ffn_w_up, ffn_w_down))
    names = ("k_sb", "v_sb", "k_diff", "v_diff", "gdn_conv", "gdn", "rwkv_shift", "rwkv")
    new_p = {n: [] for n in names}
    new_s = {n: [] for n in names}
    for l in range(depth):
        i = l // 2
        h = _ffn_half_step(h, ffn_norm[l, 0], wg[l, 0], wu[l, 0], wd[l, 0])
        if l % 2 == 0:
            prm = (ev_w_in[i], ev_w_out[i], gdn_conv_w[i], gdn_a_log[i], gdn_dt_bias[i], gdn_out_norm[i],
                   rwkv_mu[i], rwkv_w0[i], rwkv_w2[i], rwkv_a0[i], rwkv_a2[i], rwkv_g2[i],
                   rwkv_k_k[i], rwkv_k_a[i], rwkv_r_k[i], rwkv_ln_w[i], rwkv_ln_b[i])
            states = (state_gdn_conv[i], state_gdn[i], state_rwkv_shift[i], state_rwkv[i])
            h, st_p, st_s = _even_mix(h, mix_norm[l], dims, states, prm)
            for n, a_p, a_s in zip(("gdn_conv", "gdn", "rwkv_shift", "rwkv"), st_p, st_s):
                new_p[n].append(a_p)
                new_s[n].append(a_s)
        else:
            lam_init = 0.8 - 0.6 * math.exp(-0.3 * l)
            prm = (od_w_in[i], od_w_out[i], diff_lambda[i], diff_norm[i])
            caches = (cache_k_sb, cache_v_sb, cache_k_diff, cache_v_diff)
            h, kv_p, kv_s = _odd_mix(h, mix_norm[l], dims, caches, page_table, i, pos, freq, prm, lam_init)
            for n, a_p, a_s in zip(("k_sb", "v_sb", "k_diff", "v_diff"), kv_p, kv_s):
                new_p[n].append(a_p)
                new_s[n].append(a_s)
        final = final_norm if l == depth - 1 else None
        h = _ffn_half_step(h, ffn_norm[l, 1], wg[l, 1], wu[l, 1], wd[l, 1], final)

    y_prompt = h[:tp].reshape(bp, lp, d)
    y_sample = h[tp:].reshape(bs, ls, d)
    out = [y_prompt, y_sample]
    for n in ("k_sb", "v_sb", "k_diff", "v_diff", "gdn", "gdn_conv", "rwkv", "rwkv_shift"):
        out += [jnp.stack(new_p[n]), jnp.stack(new_s[n])]
    return tuple(out)
```

```python
import functools
import math
import types

import jax
import jax.numpy as jnp
import numpy as np
from jax import lax
from jax.experimental import pallas as pl
from jax.experimental.pallas import tpu as pltpu

F32 = jnp.float32
BF16 = jnp.bfloat16

HEAD_DIM = 64
HEAD_SHIFT = 6
NORM_EPS = 1e-6
L2_EPS = 1e-6
RWKV_GN_EPS = 64e-5
ROPE_DIM = HEAD_DIM // 4
ROPE_THETA = 500000.0
CONV_W = 4
ATTN_SCALE = HEAD_DIM ** -0.5
NEG_BIG = -1e30
SB_DEAD_LOG = -100.0

V7X_VMEM_LIMIT_BYTES = 56 * 1024 * 1024
SUBLANES = 8
LANES = 128


def _cparams(*sem):
    return pltpu.CompilerParams(dimension_semantics=sem, vmem_limit_bytes=V7X_VMEM_LIMIT_BYTES)


def _sigmoid(x):
    return 1.0 / (1.0 + jnp.exp(-x))


def _softplus(x):
    return jnp.maximum(x, 0.0) + jnp.log(1.0 + jnp.exp(-jnp.abs(x)))


def _rms(x, eps=NORM_EPS):
    return x * lax.rsqrt(jnp.mean(x * x, axis=-1, keepdims=True) + eps)


def _dot(a, b):
    return jnp.dot(a.astype(BF16), b.astype(BF16), preferred_element_type=F32)


def _dot_nt(a, b):
    return lax.dot_general(a.astype(BF16), b.astype(BF16), (((1,), (1,)), ((), ())),
                           preferred_element_type=F32)


def _dot_tn(a, b):
    return lax.dot_general(a.astype(BF16), b.astype(BF16), (((0,), (0,)), ((), ())),
                           preferred_element_type=F32)


def _dot_exact(a, b):
    return jnp.dot(a, b, precision=lax.Precision.HIGHEST, preferred_element_type=F32)


def _dot_split(x, m_bf16):
    hi = x.astype(BF16)
    lo = (x - hi.astype(F32)).astype(BF16)
    return (jnp.dot(hi, m_bf16, preferred_element_type=F32)
            + jnp.dot(lo, m_bf16, preferred_element_type=F32))


def _iota2(shape, dim):
    return lax.broadcasted_iota(jnp.int32, shape, dim)


def _log2(n):
    assert n > 0 and n & (n - 1) == 0, n
    return n.bit_length() - 1


def _inv_i_minus_many(n_mats, size):
    eye = (_iota2((size, size), 0) == _iota2((size, size), 1)).astype(F32)
    ts = [eye + n for n in n_mats]
    ps = list(n_mats)
    k = 1
    squared = False
    while 2 * k < size:
        if not squared:
            ps = [_dot(p, p) for p in ps]
            squared = True
        else:
            prods = [_dot(jnp.concatenate([p, t], axis=0), p) for p, t in zip(ps, ts)]
            ts = [t + pr[size:] for t, pr in zip(ts, prods)]
            ps = [pr[:size] for pr in prods]
        k *= 2
    if squared:
        ts = [t + _dot(t, p) for t, p in zip(ts, ps)]
    return ts


def _tril_block_diag(rows, chunk):
    sh = _log2(chunk)
    r = _iota2((rows, rows), 0)
    c = _iota2((rows, rows), 1)
    return jnp.where((r >> sh) == (c >> sh), jnp.where(r >= c, 1.0, 0.0), 0.0)


def _waves(chunk_subs):
    by_sub = {}
    for j, sub in enumerate(chunk_subs):
        by_sub.setdefault(sub, []).append(j)
    depth = max(len(v) for v in by_sub.values())
    return [[v[w] for v in by_sub.values() if len(v) > w] for w in range(depth)]


def _ffn_body(h_ref, g_ref, wg_ref, wu_ref, wd_ref, fg_ref, o_ref, xn_ref, acc_ref, *, apply_final):
    j = pl.program_id(1)

    @pl.when(j == 0)
    def _():
        xn_ref[...] = (_rms(h_ref[...]) * g_ref[...]).astype(BF16)
        acc_ref[...] = jnp.zeros_like(acc_ref)

    xn = xn_ref[...]
    gate = jnp.dot(xn, wg_ref[...], preferred_element_type=F32)
    up = jnp.dot(xn, wu_ref[...], preferred_element_type=F32)
    act = (gate * _sigmoid(gate)) * up
    acc_ref[...] += jnp.dot(act.astype(BF16), wd_ref[...], preferred_element_type=F32)

    @pl.when(j == pl.num_programs(1) - 1)
    def _():
        out = h_ref[...] + 0.5 * acc_ref[...]
        if apply_final:
            out = _rms(out) * fg_ref[...]
        o_ref[...] = out


def _pick_tile(n, candidates):
    for c in candidates:
        if n % c == 0:
            return c
    return n


def _ffn_half_step(h, norm_g, wg, wu, wd, final_g=None):
    t, d = h.shape
    f = wg.shape[1]
    tm = _pick_tile(t, (512, 256, 128, 64, 32, 16, 8))
    tf = _pick_tile(f, (1408, 1024, 512, 256, 128))
    apply_final = final_g is not None
    fg = final_g if apply_final else norm_g
    return pl.pallas_call(
        functools.partial(_ffn_body, apply_final=apply_final),
        grid=(t // tm, f // tf),
        in_specs=[
            pl.BlockSpec((tm, d), lambda i, j: (i, 0)),
            pl.BlockSpec((1, d), lambda i, j: (0, 0)),
            pl.BlockSpec((d, tf), lambda i, j: (0, j)),
            pl.BlockSpec((d, tf), lambda i, j: (0, j)),
            pl.BlockSpec((tf, d), lambda i, j: (j, 0)),
            pl.BlockSpec((1, d), lambda i, j: (0, 0)),
        ],
        out_specs=pl.BlockSpec((tm, d), lambda i, j: (i, 0)),
        out_shape=jax.ShapeDtypeStruct((t, d), F32),
        scratch_shapes=[pltpu.VMEM((tm, d), BF16), pltpu.VMEM((tm, d), F32)],
        compiler_params=_cparams("parallel", "arbitrary"),
        name="ffn_half_step",
    )(h, norm_g.reshape(1, d), wg, wu, wd, fg.reshape(1, d))


def _proj_even_body(h_ref, g_ref, w_ref, *o_refs, splits):
    xn = (_rms(h_ref[...]) * g_ref[...]).astype(BF16)
    p = jnp.dot(xn, w_ref[...], preferred_element_type=F32)
    off = 0
    for o_ref, width in zip(o_refs, splits):
        o_ref[...] = p[:, off:off + width]
        off += width


def _proj_even(h, norm_g, w, splits):
    t, d = h.shape
    n = w.shape[1]
    tm = _pick_tile(t, (256, 128, 64, 32, 16, 8))
    return pl.pallas_call(
        functools.partial(_proj_even_body, splits=splits),
        grid=(t // tm,),
        in_specs=[
            pl.BlockSpec((tm, d), lambda i: (i, 0)),
            pl.BlockSpec((1, d), lambda i: (0, 0)),
            pl.BlockSpec((d, n), lambda i: (0, 0)),
        ],
        out_specs=[pl.BlockSpec((tm, s), lambda i: (i, 0)) for s in splits],
        out_shape=[jax.ShapeDtypeStruct((t, s), F32) for s in splits],
        compiler_params=_cparams("parallel"),
        name="proj_even",
    )(h, norm_g.reshape(1, d), w)


def _proj_odd_body(h_ref, g_ref, w_ref, pos_ref, freq_ref, o_ref, *, rope_lo, rope_hi):
    xn = (_rms(h_ref[...]) * g_ref[...]).astype(BF16)
    p = jnp.dot(xn, w_ref[...], preferred_element_type=F32)
    o_ref[...] = p
    ang = pos_ref[...] * freq_ref[...]
    cos, sin = jnp.cos(ang), jnp.sin(ang)
    lane = _iota2(ang.shape, 1) & (HEAD_DIM - 1)
    half = ROPE_DIM // 2
    sin_lo = jnp.where(lane < half, -sin, 0.0)
    sin_hi = jnp.where(lane < half, 0.0, jnp.where(lane < ROPE_DIM, sin, 0.0))
    for c0 in range(rope_lo, rope_hi, LANES):
        x = p[:, c0:c0 + LANES]
        up = pltpu.roll(x, LANES - half, axis=1)
        down = pltpu.roll(x, half, axis=1)
        o_ref[:, c0:c0 + LANES] = x * cos + up * sin_lo + down * sin_hi


def _proj_odd(h, norm_g, w, pos, freq, rope_lo, rope_hi):
    t, d = h.shape
    n = w.shape[1]
    tm = _pick_tile(t, (256, 128, 64, 32, 16, 8))
    return pl.pallas_call(
        functools.partial(_proj_odd_body, rope_lo=rope_lo, rope_hi=rope_hi),
        grid=(t // tm,),
        in_specs=[
            pl.BlockSpec((tm, d), lambda i: (i, 0)),
            pl.BlockSpec((1, d), lambda i: (0, 0)),
            pl.BlockSpec((d, n), lambda i: (0, 0)),
            pl.BlockSpec((tm, 1), lambda i: (i, 0)),
            pl.BlockSpec((1, LANES), lambda i: (0, 0)),
        ],
        out_specs=pl.BlockSpec((tm, n), lambda i: (i, 0)),
        out_shape=jax.ShapeDtypeStruct((t, n), F32),
        compiler_params=_cparams("parallel"),
        name="proj_odd",
    )(h, norm_g.reshape(1, d), w, pos, freq)


def _outproj_body(h_ref, xa_ref, xb_ref, wa_ref, wb_ref, o_ref):
    o_ref[...] = (h_ref[...]
                  + jnp.dot(xa_ref[...].astype(BF16), wa_ref[...], preferred_element_type=F32)
                  + jnp.dot(xb_ref[...].astype(BF16), wb_ref[...], preferred_element_type=F32))


def _outproj(h, xa, xb, wa, wb):
    t, d = h.shape
    ka, kb = xa.shape[1], xb.shape[1]
    tm = _pick_tile(t, (512, 256, 128, 64, 32, 16, 8))
    return pl.pallas_call(
        _outproj_body,
        grid=(t // tm,),
        in_specs=[
            pl.BlockSpec((tm, d), lambda i: (i, 0)),
            pl.BlockSpec((tm, ka), lambda i: (i, 0)),
            pl.BlockSpec((tm, kb), lambda i: (i, 0)),
            pl.BlockSpec((ka, d), lambda i: (0, 0)),
            pl.BlockSpec((kb, d), lambda i: (0, 0)),
        ],
        out_specs=pl.BlockSpec((tm, d), lambda i: (i, 0)),
        out_shape=jax.ShapeDtypeStruct((t, d), F32),
        compiler_params=_cparams("parallel"),
        name="mix_outproj",
    )(h, xa, xb, wa, wb)


def _gdn_body(qkv_ref, z_ref, ab_ref, cst_ref, s0_ref, cw_ref, alog_ref, dtb_ref, og_ref, bd_ref, ex_ref,
              o_ref, cnew_ref, snew_ref,
              xbuf, s_scr, c_s, q_s, k_s, v_s, b_s, g_s,
              *, n_sub, tb, chunk, valid, n_heads, group_chunks):
    blk = pl.program_id(1)
    last_blk = pl.num_programs(1) - 1
    w_all = n_heads * HEAD_DIM
    lo = SUBLANES - (CONV_W - 1)
    c = chunk

    @pl.when(blk == 0)
    def _():
        for s in range(n_sub):
            xbuf[s, lo:SUBLANES, :] = cst_ref[s]
        s_scr[...] = s0_ref[...]

    cw = cw_ref[...]
    for s in range(n_sub):
        xbuf[s, SUBLANES:SUBLANES + tb, :] = qkv_ref[s * tb:(s + 1) * tb, :]
        y = xbuf[s, lo:lo + tb, :] * cw[0:1]
        for j in range(1, CONV_W):
            y = y + xbuf[s, lo + j:lo + j + tb, :] * cw[j:j + 1]
        c_s[s * tb:(s + 1) * tb, :] = y * _sigmoid(y)
        xbuf[s, lo:SUBLANES, :] = xbuf[s, lo + valid:SUBLANES + valid, :]

    @pl.when(blk == last_blk)
    def _():
        for s in range(n_sub):
            cnew_ref[s] = xbuf[s, lo:SUBLANES, :]

    bd = bd_ref[...]
    q = c_s[:, 0:w_all]
    k = c_s[:, w_all:2 * w_all]
    q_s[...] = q * lax.rsqrt(_dot_split(q * q, bd) + L2_EPS) * ATTN_SCALE
    k_s[...] = k * lax.rsqrt(_dot_split(k * k, bd) + L2_EPS)
    v_s[...] = c_s[:, 2 * w_all:3 * w_all]

    ab = ab_ref[...]
    g = -jnp.exp(alog_ref[...]) * _softplus(ab + dtb_ref[...])
    beta = _sigmoid(ab)
    if valid < tb:
        live = (_iota2(ab.shape, 0) & (tb - 1)) < valid
        g = jnp.where(live, g, 0.0)
        beta = jnp.where(live, beta, 0.0)
    g_s[...] = g
    b_s[...] = _dot_exact(beta, ex_ref[...])

    row = _iota2((c, c), 0)
    col = _iota2((c, c), 1)
    incl = row >= col
    strict = row > col
    og = og_ref[...]

    def run_group(base, chunk_subs):
        g_rows = len(chunk_subs) * c
        gc_all = _dot_exact(_tril_block_diag(g_rows, c), g_s[pl.ds(base, g_rows), :])
        gct_all = gc_all.T
        chains = []
        for j, sub in enumerate(chunk_subs):
            rs = pl.ds(base + j * c, c)
            js = slice(j * c, (j + 1) * c)
            for pair in range(n_heads // 2):
                lanes = slice(pair * LANES, (pair + 1) * LANES)
                q2, k2, v2, b2 = q_s[rs, lanes], k_s[rs, lanes], v_s[rs, lanes], b_s[rs, lanes]
                for half in range(2):
                    h = 2 * pair + half
                    hs = slice(half * HEAD_DIM, (half + 1) * HEAD_DIM)
                    ch = types.SimpleNamespace(j=j, sub=sub, h=h, pair=pair, half=half)
                    ch.q, ch.k, ch.v = q2[:, hs], k2[:, hs], v2[:, hs]
                    bh = b2[:, hs]
                    ch.gcol = gc_all[js, h:h + 1]
                    ch.glast = gc_all[(j + 1) * c - 1:(j + 1) * c, h:h + 1]
                    grow = gct_all[h:h + 1, js]
                    ch.decay = jnp.where(incl, jnp.exp(jnp.where(incl, ch.gcol - grow, 0.0)), 0.0)
                    ch.eg = jnp.exp(ch.gcol)
                    ch.kb = ch.k * bh
                    ch.vb = ch.v * bh
                    chains.append(ch)
        prods = [_dot_nt(jnp.concatenate([ch.kb, ch.q], axis=0), ch.k) for ch in chains]
        for ch, pr in zip(chains, prods):
            ch.n = -jnp.where(strict, pr[:c] * ch.decay, 0.0)
            ch.qk = jnp.where(incl, pr[c:] * ch.decay, 0.0)
        t_invs = _inv_i_minus_many([ch.n for ch in chains], c)
        sols = [_dot(t, jnp.concatenate([ch.vb, ch.kb * ch.eg], axis=1)) for t, ch in zip(t_invs, chains)]
        for ch, sol in zip(chains, sols):
            ch.u, ch.w = sol[:, :HEAD_DIM], sol[:, HEAD_DIM:]

        state = {}
        for wave in _waves(chunk_subs):
            cur = [ch for ch in chains if ch.j in wave]
            for ch in cur:
                if (ch.sub, ch.h) not in state:
                    state[(ch.sub, ch.h)] = s_scr[ch.sub, ch.h]
            ws_qs = [_dot(jnp.concatenate([ch.w, ch.q * ch.eg], axis=0), state[(ch.sub, ch.h)]) for ch in cur]
            v_news = [ch.u - x[:c] for ch, x in zip(cur, ws_qs)]
            o_parts = [_dot(ch.qk, vn) for ch, vn in zip(cur, v_news)]
            s_parts = [_dot_tn(ch.k * jnp.exp(ch.glast - ch.gcol), vn) for ch, vn in zip(cur, v_news)]
            for ch, x, op, sp in zip(cur, ws_qs, o_parts, s_parts):
                ch.o = _rms(x[c:] + op) * og
                state[(ch.sub, ch.h)] = state[(ch.sub, ch.h)] * jnp.exp(ch.glast) + sp
        for (sub, h), val in state.items():
            s_scr[sub, h] = val

        by_key = {(ch.j, ch.h): ch for ch in chains}
        for j in range(len(chunk_subs)):
            rs = pl.ds(base + j * c, c)
            for pair in range(n_heads // 2):
                lanes = slice(pair * LANES, (pair + 1) * LANES)
                z2 = z_ref[rs, lanes]
                o2 = jnp.concatenate([by_key[(j, 2 * pair)].o, by_key[(j, 2 * pair + 1)].o], axis=1)
                o_ref[rs, lanes] = o2 * (z2 * _sigmoid(z2))

    chunks_per_seq = tb // c
    if n_sub > 1 or chunks_per_seq <= group_chunks:
        run_group(0, [s for s in range(n_sub) for _ in range(chunks_per_seq)])
    else:
        g_rows = group_chunks * c

        def step(it, carry):
            run_group(pl.multiple_of(it * g_rows, g_rows), [0] * group_chunks)
            return carry

        lax.fori_loop(0, chunks_per_seq // group_chunks, step, 0)

    @pl.when(blk == last_blk)
    def _():
        snew_ref[...] = s_scr[...]


def _gdn(qkv, z, ab, conv_state, s0, conv_w, a_log, dt_bias, out_g, *, seq_rows, valid, chunk, tb, n_sub,
         group_chunks):
    n_seq, n_heads = s0.shape[0], s0.shape[1]
    w_all = n_heads * HEAD_DIM
    n_blk = seq_rows // tb
    assert n_sub == 1 or n_blk == 1
    assert n_seq % n_sub == 0 and (tb // chunk) % group_chunks == 0 or tb // chunk <= group_chunks
    _log2(tb), _log2(chunk)
    lane_h = np.arange(w_all) // HEAD_DIM
    bd = jnp.asarray(lane_h[:, None] == lane_h[None, :], BF16)
    ex = jnp.asarray((np.arange(LANES)[:, None] - n_heads) == lane_h[None, :], F32)
    row128 = lambda x: jnp.zeros((1, LANES), F32).at[0, :n_heads].set(x)
    rows_blk = n_sub * tb
    tok = lambda width: pl.BlockSpec((rows_blk, width), lambda s, b: (s * n_blk + b, 0))
    full = lambda a: pl.BlockSpec(a.shape, lambda s, b: (0,) * a.ndim)
    params = (conv_w, row128(a_log), row128(dt_bias), out_g.reshape(1, HEAD_DIM), bd, ex)
    rows = n_seq * seq_rows
    return pl.pallas_call(
        functools.partial(_gdn_body, n_sub=n_sub, tb=tb, chunk=chunk, valid=valid, n_heads=n_heads,
                          group_chunks=group_chunks),
        grid=(n_seq // n_sub, n_blk),
        in_specs=[tok(3 * w_all), tok(w_all), tok(LANES),
                  pl.BlockSpec((n_sub, CONV_W - 1, 3 * w_all), lambda s, b: (s, 0, 0)),
                  pl.BlockSpec((n_sub, n_heads, HEAD_DIM, HEAD_DIM), lambda s, b: (s, 0, 0, 0))]
        + [full(a) for a in params],
        out_specs=[tok(w_all),
                   pl.BlockSpec((n_sub, CONV_W - 1, 3 * w_all), lambda s, b: (s, 0, 0)),
                   pl.BlockSpec((n_sub, n_heads, HEAD_DIM, HEAD_DIM), lambda s, b: (s, 0, 0, 0))],
        out_shape=[jax.ShapeDtypeStruct((rows, w_all), F32),
                   jax.ShapeDtypeStruct(conv_state.shape, F32),
                   jax.ShapeDtypeStruct(s0.shape, F32)],
        scratch_shapes=[pltpu.VMEM((n_sub, SUBLANES + tb, 3 * w_all), F32),
                        pltpu.VMEM((n_sub, n_heads, HEAD_DIM, HEAD_DIM), F32),
                        pltpu.VMEM((rows_blk, 3 * w_all), F32),
                        pltpu.VMEM((rows_blk, w_all), F32), pltpu.VMEM((rows_blk, w_all), F32),
                        pltpu.VMEM((rows_blk, w_all), F32), pltpu.VMEM((rows_blk, w_all), F32),
                        pltpu.VMEM((rows_blk, LANES), F32)],
        compiler_params=_cparams("parallel", "arbitrary"),
        name="gdn_mixer",
    )(qkv, z, ab, conv_state, s0, *params)


def _rwkv_body(p_ref, sh_ref, s0_ref, mu_ref, w0_ref, a0_ref, kk_ref, ka_ref, rk_ref, lnw_ref, lnb_ref,
               w2_ref, a2_ref, g2_ref, bd_ref,
               o_ref, shnew_ref, snew_ref,
               xbuf, s_scr, r_s, k_s, v_s, kk_s, a_s, ld_s, gate_s, bonus_s,
               *, n_sub, tb, chunk, valid, n_heads, group_chunks):
    blk = pl.program_id(1)
    last_blk = pl.num_programs(1) - 1
    w_all = n_heads * HEAD_DIM
    c = chunk

    @pl.when(blk == 0)
    def _():
        for s in range(n_sub):
            xbuf[s, SUBLANES - 1:SUBLANES, :] = sh_ref[s]
        s_scr[...] = s0_ref[...]

    shifted = []
    for s in range(n_sub):
        xbuf[s, SUBLANES:SUBLANES + tb, :] = p_ref[s * tb:(s + 1) * tb, :]
        shifted.append(xbuf[s, SUBLANES - 1:SUBLANES - 1 + tb, :])
        xbuf[s, SUBLANES - 1:SUBLANES, :] = xbuf[s, SUBLANES - 1 + valid:SUBLANES + valid, :]
    shifted = shifted[0] if n_sub == 1 else jnp.concatenate(shifted, axis=0)

    @pl.when(blk == last_blk)
    def _():
        for s in range(n_sub):
            shnew_ref[s] = xbuf[s, SUBLANES - 1:SUBLANES, :]

    p = p_ref[...]
    xs = p + (shifted - p) * mu_ref[...]
    r = xs[:, 0:w_all]
    k = xs[:, w_all:2 * w_all]
    v = xs[:, 2 * w_all:3 * w_all]
    lo = xs[:, 3 * w_all:3 * w_all + LANES]
    g_lo = xs[:, 3 * w_all + LANES:]
    w_log = -_softplus(-(w0_ref[...] + _dot(jnp.tanh(lo), w2_ref[...]))) - 0.5
    log_decay = -jnp.exp(w_log)
    a = _sigmoid(a0_ref[...] + _dot(lo, a2_ref[...]))
    gate_s[...] = _dot(_sigmoid(g_lo), g2_ref[...])
    bd = bd_ref[...]
    kkx = k * kk_ref[...]
    kk = kkx * lax.rsqrt(_dot_split(kkx * kkx, bd) + L2_EPS)
    k = k * (1.0 + (a - 1.0) * ka_ref[...])
    bonus_s[...] = _dot_split(r * k * rk_ref[...], bd) * v
    if valid < tb:
        live = (_iota2(r.shape, 0) & (tb - 1)) < valid
        log_decay = jnp.where(live, log_decay, 0.0)
        kk = jnp.where(live, kk, 0.0)
        v = jnp.where(live, v, 0.0)
    r_s[...] = r
    k_s[...] = k
    v_s[...] = v
    kk_s[...] = kk
    a_s[...] = a
    ld_s[...] = log_decay

    row = _iota2((c, c), 0)
    col = _iota2((c, c), 1)
    incl = row >= col
    strict = row > col

    def run_group(base, chunk_subs):
        g_rows = len(chunk_subs) * c
        rows_g = pl.ds(base, g_rows)
        ld = ld_s[rows_g, :]
        gcum = _dot_exact(_tril_block_diag(g_rows, c), ld)
        e_pos = jnp.exp(gcum)
        e_neg = jnp.exp(-gcum)
        kk_g = kk_s[rows_g, :]
        r_t = r_s[rows_g, :] * e_pos
        b_t = kk_g * jnp.exp(gcum - ld)
        a_t = -(kk_g * a_s[rows_g, :]) * e_neg
        k_t = k_s[rows_g, :] * e_neg
        v_g = v_s[rows_g, :]
        chains = []
        for j, sub in enumerate(chunk_subs):
            js = slice(j * c, (j + 1) * c)
            e_last = e_pos[(j + 1) * c - 1:(j + 1) * c, :]
            for h in range(n_heads):
                hs = slice(h * HEAD_DIM, (h + 1) * HEAD_DIM)
                ch = types.SimpleNamespace(j=j, sub=sub, h=h)
                ch.b, ch.r, ch.v = b_t[js, hs], r_t[js, hs], v_g[js, hs]
                ch.left = jnp.concatenate([ch.b, ch.r], axis=0)
                ch.right = jnp.concatenate([a_t[js, hs], k_t[js, hs]], axis=0)
                ch.e_last = e_last[:, hs]
                chains.append(ch)
        ms = [_dot_nt(ch.left, ch.right) for ch in chains]
        for ch, m in zip(chains, ms):
            ch.a_ab = jnp.where(strict, m[:c, :c], 0.0)
            ch.a_kb = jnp.where(strict, m[:c, c:], 0.0)
            ch.q_ak = jnp.concatenate([jnp.where(incl, m[c:, :c], 0.0), jnp.where(incl, m[c:, c:], 0.0)], axis=1)
        akvs = [_dot(ch.a_kb, ch.v) for ch in chains]
        t_invs = _inv_i_minus_many([ch.a_ab for ch in chains], c)
        tbs = [_dot(t, jnp.concatenate([ch.b, akv], axis=1)) for t, ch, akv in zip(t_invs, chains, akvs)]
        for ch, x in zip(chains, tbs):
            ch.tb_r = jnp.concatenate([x[:, :HEAD_DIM], ch.r], axis=0)
            ch.takv = x[:, HEAD_DIM:]

        state = {}
        for wave in _waves(chunk_subs):
            cur = [ch for ch in chains if ch.j in wave]
            for ch in cur:
                if (ch.sub, ch.h) not in state:
                    state[(ch.sub, ch.h)] = s_scr[ch.sub, ch.h]
            lss = [_dot_nt(ch.tb_r, state[(ch.sub, ch.h)]) for ch in cur]
            uvs = [jnp.concatenate([x[:c] + ch.takv, ch.v], axis=0) for ch, x in zip(cur, lss)]
            o_parts = [_dot(ch.q_ak, uv) for ch, uv in zip(cur, uvs)]
            s_parts = [_dot_tn(uv, ch.right * ch.e_last) for ch, uv in zip(cur, uvs)]
            for ch, x, op, sp in zip(cur, lss, o_parts, s_parts):
                o = x[c:] + op
                mean = jnp.mean(o, axis=-1, keepdims=True)
                var = jnp.mean(jnp.square(o - mean), axis=-1, keepdims=True)
                ch.o = (o - mean) * lax.rsqrt(var + RWKV_GN_EPS)
                state[(ch.sub, ch.h)] = state[(ch.sub, ch.h)] * ch.e_last + sp
        for (sub, h), val in state.items():
            s_scr[sub, h] = val

        by_key = {(ch.j, ch.h): ch for ch in chains}
        for j in range(len(chunk_subs)):
            rs = pl.ds(base + j * c, c)
            for pair in range(n_heads // 2):
                lanes = slice(pair * LANES, (pair + 1) * LANES)
                on = jnp.concatenate([by_key[(j, 2 * pair)].o, by_key[(j, 2 * pair + 1)].o], axis=1)
                on = on * lnw_ref[:, lanes] + lnb_ref[:, lanes]
                o_ref[rs, lanes] = (on + bonus_s[rs, lanes]) * gate_s[rs, lanes]

    chunks_per_seq = tb // c
    if n_sub > 1 or chunks_per_seq <= group_chunks:
        run_group(0, [s for s in range(n_sub) for _ in range(chunks_per_seq)])
    else:
        g_rows = group_chunks * c

        def step(it, carry):
            run_group(pl.multiple_of(it * g_rows, g_rows), [0] * group_chunks)
            return carry

        lax.fori_loop(0, chunks_per_seq // group_chunks, step, 0)

    @pl.when(blk == last_blk)
    def _():
        snew_ref[...] = s_scr[...]


def _rwkv(p, shift_state, s0, mu, w0, w2, a0, a2, g2, k_k, k_a, r_k, ln_w, ln_b, *, seq_rows, valid, chunk, tb,
          n_sub, group_chunks):
    n_seq, n_heads = s0.shape[0], s0.shape[1]
    w_all = n_heads * HEAD_DIM
    width = p.shape[1]
    n_blk = seq_rows // tb
    assert n_sub == 1 or n_blk == 1
    _log2(tb), _log2(chunk)
    lane_h = np.arange(w_all) // HEAD_DIM
    bd = jnp.asarray(lane_h[:, None] == lane_h[None, :], BF16)
    lora = w2.shape[0]
    w2p = jnp.zeros((LANES, w_all), BF16).at[:lora].set(w2.astype(BF16))
    a2p = jnp.zeros((LANES, w_all), BF16).at[lora:lora + a2.shape[0]].set(a2.astype(BF16))
    r1 = lambda x: x.reshape(1, -1)
    params = (r1(mu), r1(w0), r1(a0), r1(k_k), r1(k_a), r1(r_k), r1(ln_w), r1(ln_b),
              w2p, a2p, g2.astype(BF16), bd)
    rows_blk = n_sub * tb
    tok = lambda wd: pl.BlockSpec((rows_blk, wd), lambda s, b: (s * n_blk + b, 0))
    full = lambda arr: pl.BlockSpec(arr.shape, lambda s, b: (0,) * arr.ndim)
    rows = n_seq * seq_rows
    return pl.pallas_call(
        functools.partial(_rwkv_body, n_sub=n_sub, tb=tb, chunk=chunk, valid=valid, n_heads=n_heads,
                          group_chunks=group_chunks),
        grid=(n_seq // n_sub, n_blk),
        in_specs=[tok(width),
                  pl.BlockSpec((n_sub, 1, width), lambda s, b: (s, 0, 0)),
                  pl.BlockSpec((n_sub, n_heads, HEAD_DIM, HEAD_DIM), lambda s, b: (s, 0, 0, 0))]
        + [full(arr) for arr in params],
        out_specs=[tok(w_all),
                   pl.BlockSpec((n_sub, 1, width), lambda s, b: (s, 0, 0)),
                   pl.BlockSpec((n_sub, n_heads, HEAD_DIM, HEAD_DIM), lambda s, b: (s, 0, 0, 0))],
        out_shape=[jax.ShapeDtypeStruct((rows, w_all), F32),
                   jax.ShapeDtypeStruct(shift_state.shape, F32),
                   jax.ShapeDtypeStruct(s0.shape, F32)],
        scratch_shapes=[pltpu.VMEM((n_sub, SUBLANES + tb, width), F32),
                        pltpu.VMEM((n_sub, n_heads, HEAD_DIM, HEAD_DIM), F32)]
        + [pltpu.VMEM((rows_blk, w_all), F32) for _ in range(8)],
        compiler_params=_cparams("parallel", "arbitrary"),
        name="rwkv7_mixer",
    )(p, shift_state, s0, *params)


def _sb_prompt_body(q_ref, k_ref, v_ref, u_ref, o_ref, *, tq, tk):
    qi = pl.program_id(2)
    n_diag = tq // tk
    lane = _iota2((tq, LANES), 1)
    q = q_ref[...] * ATTN_SCALE
    q_heads = [jnp.where(lane < HEAD_DIM, q, 0.0).astype(BF16),
               jnp.where(lane >= HEAD_DIM, q, 0.0).astype(BF16)]
    qrow = _iota2((tq, tk), 0)
    kidx = _iota2((tq, tk), 1)
    u = u_ref[...]

    def visit(k0, carry, mask):
        k = k_ref[pl.ds(k0, tk), :].astype(BF16)
        v = v_ref[pl.ds(k0, tk), :].astype(BF16)
        zs = [lax.dot_general(qh, k, (((1,), (1,)), ((), ())), preferred_element_type=F32) for qh in q_heads]
        sps = [_softplus(z) for z in zs]
        lss = [-sp if mask is None else jnp.where(mask, -sp, 0.0) for sp in sps]
        cums = [_dot_split(ls, u) for ls in lss]
        new = []
        for hh in range(2):
            c, acc = carry[2 * hh], carry[2 * hh + 1]
            w = jnp.exp(zs[hh] - sps[hh] + (c + cums[hh]))
            if mask is not None:
                w = jnp.where(mask, w, 0.0)
            acc = acc + jnp.dot(w.astype(BF16), v, preferred_element_type=F32)
            c = c + jnp.sum(lss[hh], axis=-1, keepdims=True)
            new += [c, acc]
        return tuple(new)

    carry = (jnp.zeros((tq, 1), F32), jnp.zeros((tq, LANES), F32)) * 2
    for d in range(n_diag):
        off = (n_diag - 1 - d) * tk
        k0 = pl.multiple_of(qi * tq + off, tk)
        carry = visit(k0, carry, (kidx + off) < qrow)

    def alive(state):
        jj, c0, _, c1, _ = state
        return jnp.logical_and(jj < qi * n_diag,
                               jnp.maximum(jnp.max(c0), jnp.max(c1)) > SB_DEAD_LOG)

    def body(state):
        jj = state[0]
        k0 = pl.multiple_of((qi * n_diag - 1 - jj) * tk, tk)
        return (jj + 1,) + visit(k0, state[1:], None)

    res = lax.while_loop(alive, body, (jnp.int32(0),) + carry)
    o_ref[...] = jnp.where(lane < HEAD_DIM, res[2], res[4])


def _sb_prompt(p, n_batch, seq, col_q, col_k, col_v, n_heads):
    tq = _pick_tile(seq, (256, 128, 64, 32, 16, 8))
    tk = _pick_tile(tq, (128, 64, 32, 16, 8))
    n_q = seq // tq
    u = jnp.asarray(np.arange(tk)[:, None] > np.arange(tk)[None, :], BF16)
    return pl.pallas_call(
        functools.partial(_sb_prompt_body, tq=tq, tk=tk),
        grid=(n_batch, n_heads // 2, n_q),
        in_specs=[
            pl.BlockSpec((tq, LANES), lambda b, hp, i: (b * n_q + i, col_q + hp)),
            pl.BlockSpec((seq, LANES), lambda b, hp, i: (b, col_k + hp)),
            pl.BlockSpec((seq, LANES), lambda b, hp, i: (b, col_v + hp)),
            pl.BlockSpec((tk, tk), lambda b, hp, i: (0, 0)),
        ],
        out_specs=pl.BlockSpec((tq, LANES), lambda b, hp, i: (b * n_q + i, hp)),
        out_shape=jax.ShapeDtypeStruct((n_batch * seq, n_heads * HEAD_DIM), F32),
        compiler_params=_cparams("parallel", "parallel", "arbitrary"),
        name="sb_attn_prompt",
    )(p, p, p, u)


def _lambda_of(lam_ref, lam_init):
    lv = lam_ref[...]
    return (jnp.exp(jnp.sum(lv[0:1] * lv[1:2], axis=-1, keepdims=True))
            - jnp.exp(jnp.sum(lv[2:3] * lv[3:4], axis=-1, keepdims=True)) + lam_init)


def _diff_prompt_body(q_ref, k_ref, v_ref, lam_ref, sn_ref, o_ref, *, tq, tk, lam_init):
    qi = pl.program_id(2)
    lane = _iota2((tq, LANES), 1)
    q = q_ref[...] * ATTN_SCALE
    q_maps = [jnp.where(lane < HEAD_DIM, q, 0.0).astype(BF16),
              jnp.where(lane >= HEAD_DIM, q, 0.0).astype(BF16)]

    def visit(k0, carry, mask):
        k = k_ref[pl.ds(k0, tk), :].astype(BF16)
        v = v_ref[pl.ds(k0, tk), :].astype(BF16)
        ss = [lax.dot_general(qm, k, (((1,), (1,)), ((), ())), preferred_element_type=F32) for qm in q_maps]
        if mask is not None:
            ss = [jnp.where(mask, s, NEG_BIG) for s in ss]
        m_news = [jnp.maximum(carry[3 * mm], jnp.max(ss[mm], axis=-1, keepdims=True)) for mm in range(2)]
        prs = [jnp.exp(ss[mm] - m_news[mm]) for mm in range(2)]
        if mask is not None:
            prs = [jnp.where(mask, pr, 0.0) for pr in prs]
        pvs = [jnp.dot(pr.astype(BF16), v, preferred_element_type=F32) for pr in prs]
        new = []
        for mm in range(2):
            m_run, l_run, acc = carry[3 * mm:3 * mm + 3]
            corr = jnp.exp(m_run - m_news[mm])
            new += [m_news[mm], l_run * corr + jnp.sum(prs[mm], axis=-1, keepdims=True), acc * corr + pvs[mm]]
        return tuple(new)

    one = (jnp.full((tq, 1), NEG_BIG, F32), jnp.zeros((tq, 1), F32), jnp.zeros((tq, LANES), F32))
    n_full = (qi * tq) // tk
    res = lax.fori_loop(0, n_full, lambda kb, cr: visit(pl.multiple_of(kb * tk, tk), cr, None), one * 2)
    k0 = pl.multiple_of(n_full * tk, tk)
    mask = (k0 + _iota2((tq, tk), 1)) <= (qi * tq + _iota2((tq, tk), 0))
    res = visit(k0, res, mask)
    lam = _lambda_of(lam_ref, lam_init)
    o = res[2] / res[1] - lam * (res[5] / res[4])
    o_ref[...] = _rms(o) * sn_ref[...] * (1.0 - lam_init)


def _diff_prompt(p, n_batch, seq, col_q, col_k, col_v, n_heads, lam_vecs, sub_norm, lam_init):
    tq = _pick_tile(seq, (256, 128, 64, 32, 16, 8))
    tk = _pick_tile(seq, (512, 256, 128, 64, 32, 16, 8))
    assert tk % tq == 0
    n_q = seq // tq
    return pl.pallas_call(
        functools.partial(_diff_prompt_body, tq=tq, tk=tk, lam_init=lam_init),
        grid=(n_batch, n_heads, n_q),
        in_specs=[
            pl.BlockSpec((tq, LANES), lambda b, h, i: (b * n_q + i, col_q + h)),
            pl.BlockSpec((seq, LANES), lambda b, h, i: (b, col_k + h)),
            pl.BlockSpec((seq, LANES), lambda b, h, i: (b, col_v + h)),
            pl.BlockSpec(lam_vecs.shape, lambda b, h, i: (0, 0)),
            pl.BlockSpec((1, LANES), lambda b, h, i: (0, 0)),
        ],
        out_specs=pl.BlockSpec((tq, LANES), lambda b, h, i: (b * n_q + i, h)),
        out_shape=jax.ShapeDtypeStruct((n_batch * seq, n_heads * LANES), F32),
        compiler_params=_cparams("parallel", "parallel", "arbitrary"),
        name="diff_attn_prompt",
    )(p, p, p, lam_vecs, sub_norm.reshape(1, LANES))


def _block_diag_queries(q, n_groups):
    lane_g = _iota2(q.shape, 1) >> HEAD_SHIFT
    return jnp.concatenate([jnp.where(lane_g == g, q, 0.0) for g in range(n_groups)], axis=0)


def _sb_sample_body(pt_ref, q_ref, kn_ref, vn_ref, *rest, n_new, n_groups, n_pages):
    del pt_ref
    kt_refs, vt_refs = rest[:n_pages], rest[n_pages:2 * n_pages]
    u_ref, o_ref = rest[2 * n_pages:]
    rows = n_groups * SUBLANES
    width = n_groups * HEAD_DIM
    qrow = _iota2((rows, 1), 0) & (SUBLANES - 1)

    qbd = _block_diag_queries(q_ref[...] * ATTN_SCALE, n_groups)
    c = jnp.zeros((rows, 1), F32)
    acc = jnp.zeros((rows, width), F32)
    kn = kn_ref[...]
    vn = vn_ref[...]
    for t in range(n_new - 1, -1, -1):
        z = jnp.sum(qbd * kn[t:t + 1, :], axis=-1, keepdims=True)
        vis = t < qrow
        sp = _softplus(z)
        acc = acc + jnp.where(vis, jnp.exp(z - sp + c), 0.0) * vn[t:t + 1, :]
        c = c + jnp.where(vis, -sp, 0.0)

    page = lambda x, j: x[:, j * LANES:(j + 1) * LANES]
    kt_all = jnp.concatenate([r[...].reshape(width, LANES).astype(BF16) for r in kt_refs], axis=1)
    vt_all = jnp.concatenate([r[...].reshape(width, LANES).astype(BF16) for r in vt_refs], axis=1)
    z = jnp.dot(qbd.astype(BF16), kt_all, preferred_element_type=F32)
    sp = _softplus(z)
    sp_rows = jnp.concatenate([page(sp, j) for j in range(n_pages)], axis=0)
    cum_rows = _dot_split(sp_rows, u_ref[...])
    tot_rows = jnp.sum(sp_rows, axis=-1, keepdims=True)
    ws = [None] * n_pages
    for j in range(n_pages - 1, -1, -1):
        rj = slice(j * rows, (j + 1) * rows)
        ws[j] = jnp.exp(page(z, j) - page(sp, j) + (c - cum_rows[rj, :]))
        c = c - tot_rows[rj, :]
    acc = acc + lax.dot_general(jnp.concatenate(ws, axis=1).astype(BF16), vt_all, (((1,), (1,)), ((), ())),
                                preferred_element_type=F32)

    lane_g = _iota2((SUBLANES, width), 1) >> HEAD_SHIFT
    out = jnp.zeros((SUBLANES, width), F32)
    for g in range(n_groups):
        out = out + jnp.where(lane_g == g, acc[g * SUBLANES:(g + 1) * SUBLANES, :], 0.0)
    o_ref[...] = out


def _sb_sample(page_table, q8, kn8, vn8, cache_kt, cache_vt, layer, n_new):
    n_b, n_pages = page_table.shape
    n_heads = cache_kt.shape[2]
    width = n_heads * HEAD_DIM
    page = cache_kt.shape[-1]
    u = jnp.asarray(np.arange(page)[:, None] > np.arange(page)[None, :], BF16)
    tok = pl.BlockSpec((None, SUBLANES, width), lambda b, pt: (b, 0, 0))
    pg = [pl.BlockSpec((None, None, n_heads, HEAD_DIM, page), lambda b, pt, j=j: (layer, pt[b, j], 0, 0, 0))
          for j in range(n_pages)]
    return pl.pallas_call(
        functools.partial(_sb_sample_body, n_new=n_new, n_groups=n_heads, n_pages=n_pages),
        grid_spec=pltpu.PrefetchScalarGridSpec(
            num_scalar_prefetch=1,
            grid=(n_b,),
            in_specs=[tok, tok, tok] + pg + pg + [pl.BlockSpec((page, page), lambda b, pt: (0, 0))],
            out_specs=tok),
        out_shape=jax.ShapeDtypeStruct((n_b, SUBLANES, width), F32),
        compiler_params=_cparams("parallel"),
        name="sb_attn_sample",
    )(page_table, q8, kn8, vn8, *([cache_kt] * n_pages), *([cache_vt] * n_pages), u)


def _diff_sample_body(pt_ref, q_ref, kn_ref, vn_ref, *rest, n_new, n_heads, n_pages, lam_init):
    del pt_ref
    kt_refs, v_refs = rest[:n_pages], rest[n_pages:2 * n_pages]
    ex_ref, lam_ref, sn_ref, o_ref = rest[2 * n_pages:]
    n_groups = 2 * n_heads
    rows = n_groups * SUBLANES
    hrows = 2 * SUBLANES
    qrow = _iota2((rows, 1), 0) & (SUBLANES - 1)

    qbd = _block_diag_queries(q_ref[...] * ATTN_SCALE, n_groups)
    kn = kn_ref[...]
    vn = vn_ref[...]
    m_run = jnp.full((rows, 1), NEG_BIG, F32)
    l_run = jnp.zeros((rows, 1), F32)
    acc = jnp.zeros((rows, LANES), F32)
    for t in range(n_new):
        s = jnp.sum(qbd * kn[t:t + 1, :], axis=-1, keepdims=True)
        vis = t <= qrow
        m_new = jnp.maximum(m_run, jnp.where(vis, s, NEG_BIG))
        corr = jnp.exp(m_run - m_new)
        pr = jnp.where(vis, jnp.exp(s - m_new), 0.0)
        v_rows = jnp.concatenate(
            [jnp.broadcast_to(vn[t:t + 1, h * LANES:(h + 1) * LANES], (hrows, LANES)) for h in range(n_heads)],
            axis=0)
        l_run = l_run * corr + pr
        acc = acc * corr + pr * v_rows
        m_run = m_new

    kt_all = jnp.concatenate([r[...].reshape(n_groups * HEAD_DIM, LANES).astype(BF16) for r in kt_refs], axis=1)
    s = jnp.dot(qbd.astype(BF16), kt_all, preferred_element_type=F32)
    m_new = jnp.maximum(m_run, jnp.max(s, axis=-1, keepdims=True))
    corr = jnp.exp(m_run - m_new)
    pr = jnp.exp(s - m_new)
    l_run = l_run * corr + jnp.sum(pr, axis=-1, keepdims=True)
    pr_rows = jnp.concatenate([pr[:, j * LANES:(j + 1) * LANES] for j in range(n_pages)], axis=0)
    spread = jnp.dot(pr_rows.astype(BF16), ex_ref[...], preferred_element_type=F32)
    own = ((_iota2((rows, LANES * n_heads), 1) & (n_heads - 1))
           == (_iota2((rows, LANES * n_heads), 0) >> _log2(hrows)))
    pr_exp = jnp.concatenate(
        [jnp.where(own, spread[j * rows:(j + 1) * rows, :], 0.0).astype(BF16) for j in range(n_pages)], axis=1)
    v_all = jnp.concatenate([r[...].astype(BF16) for r in v_refs], axis=0)
    o_all = (acc * corr + jnp.dot(pr_exp, v_all, preferred_element_type=F32)) / l_run
    lam = _lambda_of(lam_ref, lam_init)
    outs = []
    for h in range(n_heads):
        o = o_all[h * hrows:h * hrows + SUBLANES, :] - lam * o_all[h * hrows + SUBLANES:(h + 1) * hrows, :]
        outs.append(_rms(o) * sn_ref[...] * (1.0 - lam_init))
    o_ref[...] = jnp.concatenate(outs, axis=1)


def _diff_sample(page_table, q8, kn8, vn8, cache_kt, cache_v, layer, n_new, lam_vecs, sub_norm, lam_init):
    n_b, n_pages = page_table.shape
    n_heads = cache_kt.shape[2]
    page = cache_kt.shape[-1]
    width = n_heads * 2 * HEAD_DIM
    tok = pl.BlockSpec((None, SUBLANES, width), lambda b, pt: (b, 0, 0))
    kpg = [pl.BlockSpec((None, None, n_heads, 2, HEAD_DIM, page), lambda b, pt, j=j: (layer, pt[b, j], 0, 0, 0, 0))
           for j in range(n_pages)]
    vpg = [pl.BlockSpec((None, None, page * n_heads, LANES), lambda b, pt, j=j: (layer, pt[b, j], 0, 0))
           for j in range(n_pages)]
    _log2(n_heads)
    ex = jnp.asarray(np.arange(page)[:, None] == (np.arange(page * n_heads)[None, :] // n_heads), BF16)
    cache_v2 = cache_v.reshape(cache_v.shape[:2] + (page * n_heads, LANES))
    return pl.pallas_call(
        functools.partial(_diff_sample_body, n_new=n_new, n_heads=n_heads, n_pages=n_pages, lam_init=lam_init),
        grid_spec=pltpu.PrefetchScalarGridSpec(
            num_scalar_prefetch=1,
            grid=(n_b,),
            in_specs=[tok, tok, tok] + kpg + vpg
            + [pl.BlockSpec(ex.shape, lambda b, pt: (0, 0)),
               pl.BlockSpec(lam_vecs.shape, lambda b, pt: (0, 0)), pl.BlockSpec((1, LANES), lambda b, pt: (0, 0))],
            out_specs=tok),
        out_shape=jax.ShapeDtypeStruct((n_b, SUBLANES, width), F32),
        compiler_params=_cparams("parallel"),
        name="diff_attn_sample",
    )(page_table, q8, kn8, vn8, *([cache_kt] * n_pages), *([cache_v2] * n_pages),
      ex, lam_vecs, sub_norm.reshape(1, LANES))


def _pad_rows(x, n_seq, seq, to):
    w = x.shape[-1]
    return jnp.pad(x.reshape(n_seq, seq, w), ((0, 0), (0, to - seq), (0, 0))).reshape(n_seq * to, w)


def _unpad_rows(x, n_seq, seq, to):
    return x.reshape(n_seq, to, x.shape[-1])[:, :seq].reshape(n_seq * seq, x.shape[-1])


def _even_mix(h, norm_g, dims, states, prm):
    bp, lp, bs, ls = dims
    tp = bp * lp
    (w_in, w_out, conv_w, a_log, dt_bias, out_g, mu, w0, w2, a0, a2, g2, k_k, k_a, r_k, ln_w, ln_b) = prm
    conv_s, gdn_s, shift_s, rwkv_s = states
    n_heads = a_log.shape[0]
    w_gdn = n_heads * HEAD_DIM
    gdn_in = 4 * w_gdn + 2 * n_heads
    ab_w = jnp.zeros((w_in.shape[0], LANES), F32).at[:, :2 * n_heads].set(w_in[:, 4 * w_gdn:gdn_in])
    w_cat = jnp.concatenate([w_in[:, :4 * w_gdn], w_in[:, gdn_in:], ab_w], axis=1).astype(BF16)
    rwkv_in = w_in.shape[1] - gdn_in
    qkv, z, p_rwkv, ab = _proj_even(h, norm_g, w_cat, (3 * w_gdn, w_gdn, rwkv_in, LANES))

    chunk_p = min(64, lp)
    tb_p = _pick_tile(lp, (256, 128, 64)) if lp >= 64 else lp
    cfg_p = dict(seq_rows=lp, valid=tb_p, chunk=chunk_p, tb=tb_p, n_sub=1, group_chunks=2)
    zeros = lambda shape: jnp.zeros(shape, F32)
    o_a_p, conv_p, gdn_p = _gdn(qkv, z, ab, zeros((bp,) + conv_s.shape[1:]), zeros((bp,) + gdn_s.shape[1:]),
                                conv_w, a_log, dt_bias, out_g, **cfg_p)
    o_b_p, shift_p, rwkv_p = _rwkv(p_rwkv, zeros((bp,) + shift_s.shape[1:]), zeros((bp,) + rwkv_s.shape[1:]),
                                   mu, w0, w2, a0, a2, g2, k_k, k_a, r_k, ln_w, ln_b, **cfg_p)

    ls8 = -(-ls // SUBLANES) * SUBLANES
    cfg_s = dict(seq_rows=ls8, valid=ls, chunk=ls8, tb=ls8, n_sub=_pick_tile(bs, (4, 2, 1)), group_chunks=1)
    pad = lambda x: _pad_rows(x[tp:], bs, ls, ls8)
    o_a_s, conv_n, gdn_n = _gdn(pad(qkv), pad(z), pad(ab), conv_s, gdn_s, conv_w, a_log, dt_bias, out_g, **cfg_s)
    o_b_s, shift_n, rwkv_n = _rwkv(pad(p_rwkv), shift_s, rwkv_s, mu, w0, w2, a0, a2, g2, k_k, k_a, r_k,
                                   ln_w, ln_b, **cfg_s)
    o_a = jnp.concatenate([o_a_p, _unpad_rows(o_a_s, bs, ls, ls8)], axis=0)
    o_b = jnp.concatenate([o_b_p, _unpad_rows(o_b_s, bs, ls, ls8)], axis=0)
    w_out_b = w_out.astype(BF16)
    h = _outproj(h, o_a, o_b, w_out_b[:w_gdn], w_out_b[w_gdn:])
    return h, (conv_p, gdn_p, shift_p, rwkv_p), (conv_n, gdn_n, shift_n, rwkv_n)


def _odd_mix(h, norm_g, dims, caches, page_table, layer_idx, pos, freq, prm, lam_init):
    bp, lp, bs, ls = dims
    tp = bp * lp
    w_in, w_out, lam_vecs, sub_norm = prm
    ck_sb, cv_sb, ck_d, cv_d = caches
    h_sb = ck_sb.shape[3]
    h_d = ck_d.shape[3]
    w_sb = h_sb * HEAD_DIM
    w_d = h_d * 2 * HEAD_DIM
    p = _proj_odd(h, norm_g, w_in.astype(BF16), pos, freq, 3 * w_sb, 3 * w_sb + 2 * w_d)
    cb = lambda col: col // LANES
    o_sb_p = _sb_prompt(p, bp, lp, cb(0), cb(w_sb), cb(2 * w_sb), h_sb)
    o_d_p = _diff_prompt(p, bp, lp, cb(3 * w_sb), cb(3 * w_sb + w_d), cb(3 * w_sb + 2 * w_d), h_d,
                         lam_vecs, sub_norm, lam_init)

    ps = p[tp:]
    col = lambda a, b: _pad_rows(ps[:, a:b], bs, ls, SUBLANES).reshape(bs, SUBLANES, b - a)
    o_sb_s = _sb_sample(page_table, col(0, w_sb), col(w_sb, 2 * w_sb), col(2 * w_sb, 3 * w_sb),
                        jnp.transpose(ck_sb, (0, 1, 3, 4, 2)), jnp.transpose(cv_sb, (0, 1, 3, 4, 2)),
                        layer_idx, ls)
    d0 = 3 * w_sb
    o_d_s = _diff_sample(page_table, col(d0, d0 + w_d), col(d0 + w_d, d0 + 2 * w_d), col(d0 + 2 * w_d, d0 + 3 * w_d),
                         jnp.transpose(ck_d, (0, 1, 3, 4, 5, 2)), cv_d, layer_idx, ls, lam_vecs, sub_norm, lam_init)
    unp = lambda x: x[:, :ls].reshape(bs * ls, x.shape[-1])
    o_sb = jnp.concatenate([o_sb_p, unp(o_sb_s)], axis=0)
    o_d = jnp.concatenate([o_d_p, unp(o_d_s)], axis=0)
    w_out_b = w_out.astype(BF16)
    h = _outproj(h, o_sb, o_d, w_out_b[:w_sb], w_out_b[w_sb:])

    def split(rows, b, l):
        k_sb = rows[:, w_sb:2 * w_sb].reshape(b, l, h_sb, HEAD_DIM)
        v_sb = rows[:, 2 * w_sb:3 * w_sb].reshape(b, l, h_sb, HEAD_DIM)
        k_d = rows[:, d0 + w_d:d0 + 2 * w_d].reshape(b, l, h_d, 2, HEAD_DIM)
        v_d = rows[:, d0 + 2 * w_d:d0 + 3 * w_d].reshape(b, l, h_d, 2 * HEAD_DIM)
        return k_sb, v_sb, k_d, v_d

    return h, split(p[:tp], bp, lp), split(ps, bs, ls)


def kernel(x_prompt, x_sample, cache_k_sb, cache_v_sb, cache_k_diff, cache_v_diff, page_table, state_gdn, state_gdn_conv, state_rwkv, state_rwkv_shift, ffn_norm, ffn_w_gate, ffn_w_up, ffn_w_down, mix_norm, final_norm, ev_w_in, ev_w_out, gdn_conv_w, gdn_a_log, gdn_dt_bias, gdn_out_norm, rwkv_mu, rwkv_w0, rwkv_w2, rwkv_a0, rwkv_a2, rwkv_g2, rwkv_k_k, rwkv_k_a, rwkv_r_k, rwkv_ln_w, rwkv_ln_b, od_w_in, od_w_out, diff_lambda, diff_norm):
    bp, lp, d = x_prompt.shape
    bs, ls, _ = x_sample.shape
    depth = ffn_norm.shape[0]
    dims = (bp, lp, bs, ls)
    tp = bp * lp
    past_len = page_table.shape[1] * cache_k_sb.shape[2]

    h = jnp.concatenate([x_prompt.reshape(tp, d), x_sample.reshape(bs * ls, d)], axis=0)
    pos = jnp.concatenate([jnp.tile(jnp.arange(lp), bp), jnp.tile(past_len + jnp.arange(ls), bs)])
    pos = pos.astype(F32).reshape(-1, 1)
    half = ROPE_DIM // 2
    inv_freq = ROPE_THETA ** (-jnp.arange(half, dtype=F32) * 2.0 / ROPE_DIM)
    lane = np.arange(LANES) % HEAD_DIM
    freq = jnp.where(lane < ROPE_DIM, inv_freq[lane % half], 0.0).astype(F32).reshape(1, LANES)

    wg, wu, wd = (w.astype(BF16) for w in (ffn_w_gate, ffn_w_up, ffn_w_down))
    names = ("k_sb", "v_sb", "k_diff", "v_diff", "gdn_conv", "gdn", "rwkv_shift", "rwkv")
    new_p = {n: [] for n in names}
    new_s = {n: [] for n in names}
    for l in range(depth):
        i = l // 2
        h = _ffn_half_step(h, ffn_norm[l, 0], wg[l, 0], wu[l, 0], wd[l, 0])
        if l % 2 == 0:
            prm = (ev_w_in[i], ev_w_out[i], gdn_conv_w[i], gdn_a_log[i], gdn_dt_bias[i], gdn_out_norm[i],
                   rwkv_mu[i], rwkv_w0[i], rwkv_w2[i], rwkv_a0[i], rwkv_a2[i], rwkv_g2[i],
                   rwkv_k_k[i], rwkv_k_a[i], rwkv_r_k[i], rwkv_ln_w[i], rwkv_ln_b[i])
            states = (state_gdn_conv[i], state_gdn[i], state_rwkv_shift[i], state_rwkv[i])
            h, st_p, st_s = _even_mix(h, mix_norm[l], dims, states, prm)
            for n, a_p, a_s in zip(("gdn_conv", "gdn", "rwkv_shift", "rwkv"), st_p, st_s):
                new_p[n].append(a_p)
                new_s[n].append(a_s)
        else:
            lam_init = 0.8 - 0.6 * math.exp(-0.3 * l)
            prm = (od_w_in[i], od_w_out[i], diff_lambda[i], diff_norm[i])
            caches = (cache_k_sb, cache_v_sb, cache_k_diff, cache_v_diff)
            h, kv_p, kv_s = _odd_mix(h, mix_norm[l], dims, caches, page_table, i, pos, freq, prm, lam_init)
            for n, a_p, a_s in zip(("k_sb", "v_sb", "k_diff", "v_diff"), kv_p, kv_s):
                new_p[n].append(a_p)
                new_s[n].append(a_s)
        final = final_norm if l == depth - 1 else None
        h = _ffn_half_step(h, ffn_norm[l, 1], wg[l, 1], wu[l, 1], wd[l, 1], final)

    y_prompt = h[:tp].reshape(bp, lp, d)
    y_sample = h[tp:].reshape(bs, ls, d)
    out = [y_prompt, y_sample]
    for n in ("k_sb", "v_sb", "k_diff", "v_diff", "gdn", "gdn_conv", "rwkv", "rwkv_shift"):
        out += [jnp.stack(new_p[n]), jnp.stack(new_s[n])]
    return tuple(out)
```

```python
import functools
import math
import types

import jax
import jax.numpy as jnp
import numpy as np
from jax import lax
from jax.experimental import pallas as pl
from jax.experimental.pallas import tpu as pltpu

F32 = jnp.float32
BF16 = jnp.bfloat16

HEAD_DIM = 64
HEAD_SHIFT = 6
NORM_EPS = 1e-6
L2_EPS = 1e-6
RWKV_GN_EPS = 64e-5
ROPE_DIM = HEAD_DIM // 4
ROPE_THETA = 500000.0
CONV_W = 4
ATTN_SCALE = HEAD_DIM ** -0.5
NEG_BIG = -1e30
SB_DEAD_LOG = -100.0

V7X_VMEM_LIMIT_BYTES = 56 * 1024 * 1024
SUBLANES = 8
LANES = 128


def _cparams(*sem):
    return pltpu.CompilerParams(dimension_semantics=sem, vmem_limit_bytes=V7X_VMEM_LIMIT_BYTES)


def _sigmoid(x):
    return 1.0 / (1.0 + jnp.exp(-x))


def _softplus(x):
    return jnp.maximum(x, 0.0) + jnp.log(1.0 + jnp.exp(-jnp.abs(x)))


def _rms(x, eps=NORM_EPS):
    return x * lax.rsqrt(jnp.mean(x * x, axis=-1, keepdims=True) + eps)


def _dot(a, b):
    return jnp.dot(a.astype(BF16), b.astype(BF16), preferred_element_type=F32)


def _dot_nt(a, b):
    return lax.dot_general(a.astype(BF16), b.astype(BF16), (((1,), (1,)), ((), ())),
                           preferred_element_type=F32)


def _dot_tn(a, b):
    return lax.dot_general(a.astype(BF16), b.astype(BF16), (((0,), (0,)), ((), ())),
                           preferred_element_type=F32)


def _dot_exact(a, b):
    return jnp.dot(a, b, precision=lax.Precision.HIGHEST, preferred_element_type=F32)


def _dot_split(x, m_bf16):
    hi = x.astype(BF16)
    lo = (x - hi.astype(F32)).astype(BF16)
    return (jnp.dot(hi, m_bf16, preferred_element_type=F32)
            + jnp.dot(lo, m_bf16, preferred_element_type=F32))


def _iota2(shape, dim):
    return lax.broadcasted_iota(jnp.int32, shape, dim)


def _log2(n):
    assert n > 0 and n & (n - 1) == 0, n
    return n.bit_length() - 1


def _inv_i_minus_many(n_mats, size):
    eye = (_iota2((size, size), 0) == _iota2((size, size), 1)).astype(F32)
    ts = [eye + n for n in n_mats]
    ps = list(n_mats)
    k = 1
    squared = False
    while 2 * k < size:
        if not squared:
            ps = [_dot(p, p) for p in ps]
            squared = True
        else:
            prods = [_dot(jnp.concatenate([p, t], axis=0), p) for p, t in zip(ps, ts)]
            ts = [t + pr[size:] for t, pr in zip(ts, prods)]
            ps = [pr[:size] for pr in prods]
        k *= 2
    if squared:
        ts = [t + _dot(t, p) for t, p in zip(ts, ps)]
    return ts


def _tril_block_diag(rows, chunk):
    sh = _log2(chunk)
    r = _iota2((rows, rows), 0)
    c = _iota2((rows, rows), 1)
    return jnp.where((r >> sh) == (c >> sh), jnp.where(r >= c, 1.0, 0.0), 0.0)


def _waves(chunk_subs):
    by_sub = {}
    for j, sub in enumerate(chunk_subs):
        by_sub.setdefault(sub, []).append(j)
    depth = max(len(v) for v in by_sub.values())
    return [[v[w] for v in by_sub.values() if len(v) > w] for w in range(depth)]


def _ffn_body(h_ref, g_ref, wg_ref, wu_ref, wd_ref, fg_ref, o_ref, xn_ref, acc_ref, *, apply_final):
    j = pl.program_id(1)

    @pl.when(j == 0)
    def _():
        xn_ref[...] = (_rms(h_ref[...]) * g_ref[...]).astype(BF16)
        acc_ref[...] = jnp.zeros_like(acc_ref)

    xn = xn_ref[...]
    gate = jnp.dot(xn, wg_ref[...], preferred_element_type=F32)
    up = jnp.dot(xn, wu_ref[...], preferred_element_type=F32)
    act = (gate * _sigmoid(gate)) * up
    acc_ref[...] += jnp.dot(act.astype(BF16), wd_ref[...], preferred_element_type=F32)

    @pl.when(j == pl.num_programs(1) - 1)
    def _():
        out = h_ref[...] + 0.5 * acc_ref[...]
        if apply_final:
            out = _rms(out) * fg_ref[...]
        o_ref[...] = out


def _pick_tile(n, candidates):
    for c in candidates:
        if n % c == 0:
            return c
    return n


def _ffn_half_step(h, norm_g, wg, wu, wd, final_g=None):
    t, d = h.shape
    f = wg.shape[1]
    tm = _pick_tile(t, (512, 256, 128, 64, 32, 16, 8))
    tf = _pick_tile(f, (1408, 1024, 512, 256, 128))
    apply_final = final_g is not None
    fg = final_g if apply_final else norm_g
    return pl.pallas_call(
        functools.partial(_ffn_body, apply_final=apply_final),
        grid=(t // tm, f // tf),
        in_specs=[
            pl.BlockSpec((tm, d), lambda i, j: (i, 0)),
            pl.BlockSpec((1, d), lambda i, j: (0, 0)),
            pl.BlockSpec((d, tf), lambda i, j: (0, j)),
            pl.BlockSpec((d, tf), lambda i, j: (0, j)),
            pl.BlockSpec((tf, d), lambda i, j: (j, 0)),
            pl.BlockSpec((1, d), lambda i, j: (0, 0)),
        ],
        out_specs=pl.BlockSpec((tm, d), lambda i, j: (i, 0)),
        out_shape=jax.ShapeDtypeStruct((t, d), F32),
        scratch_shapes=[pltpu.VMEM((tm, d), BF16), pltpu.VMEM((tm, d), F32)],
        compiler_params=_cparams("parallel", "arbitrary"),
        name="ffn_half_step",
    )(h, norm_g.reshape(1, d), wg, wu, wd, fg.reshape(1, d))


def _proj_even_body(h_ref, g_ref, w_ref, *o_refs, splits):
    xn = (_rms(h_ref[...]) * g_ref[...]).astype(BF16)
    p = jnp.dot(xn, w_ref[...], preferred_element_type=F32)
    off = 0
    for o_ref, width in zip(o_refs, splits):
        o_ref[...] = p[:, off:off + width]
        off += width


def _proj_even(h, norm_g, w, splits):
    t, d = h.shape
    n = w.shape[1]
    tm = _pick_tile(t, (256, 128, 64, 32, 16, 8))
    return pl.pallas_call(
        functools.partial(_proj_even_body, splits=splits),
        grid=(t // tm,),
        in_specs=[
            pl.BlockSpec((tm, d), lambda i: (i, 0)),
            pl.BlockSpec((1, d), lambda i: (0, 0)),
            pl.BlockSpec((d, n), lambda i: (0, 0)),
        ],
        out_specs=[pl.BlockSpec((tm, s), lambda i: (i, 0)) for s in splits],
        out_shape=[jax.ShapeDtypeStruct((t, s), F32) for s in splits],
        compiler_params=_cparams("parallel"),
        name="proj_even",
    )(h, norm_g.reshape(1, d), w)


def _proj_odd_body(h_ref, g_ref, w_ref, pos_ref, freq_ref, o_ref, *, rope_lo, rope_hi):
    xn = (_rms(h_ref[...]) * g_ref[...]).astype(BF16)
    p = jnp.dot(xn, w_ref[...], preferred_element_type=F32)
    o_ref[...] = p
    ang = pos_ref[...] * freq_ref[...]
    cos, sin = jnp.cos(ang), jnp.sin(ang)
    lane = _iota2(ang.shape, 1) & (HEAD_DIM - 1)
    half = ROPE_DIM // 2
    sin_lo = jnp.where(lane < half, -sin, 0.0)
    sin_hi = jnp.where(lane < half, 0.0, jnp.where(lane < ROPE_DIM, sin, 0.0))
    for c0 in range(rope_lo, rope_hi, LANES):
        x = p[:, c0:c0 + LANES]
        up = pltpu.roll(x, LANES - half, axis=1)
        down = pltpu.roll(x, half, axis=1)
        o_ref[:, c0:c0 + LANES] = x * cos + up * sin_lo + down * sin_hi


def _proj_odd(h, norm_g, w, pos, freq, rope_lo, rope_hi):
    t, d = h.shape
    n = w.shape[1]
    tm = _pick_tile(t, (256, 128, 64, 32, 16, 8))
    return pl.pallas_call(
        functools.partial(_proj_odd_body, rope_lo=rope_lo, rope_hi=rope_hi),
        grid=(t // tm,),
        in_specs=[
            pl.BlockSpec((tm, d), lambda i: (i, 0)),
            pl.BlockSpec((1, d), lambda i: (0, 0)),
            pl.BlockSpec((d, n), lambda i: (0, 0)),
            pl.BlockSpec((tm, 1), lambda i: (i, 0)),
            pl.BlockSpec((1, LANES), lambda i: (0, 0)),
        ],
        out_specs=pl.BlockSpec((tm, n), lambda i: (i, 0)),
        out_shape=jax.ShapeDtypeStruct((t, n), F32),
        compiler_params=_cparams("parallel"),
        name="proj_odd",
    )(h, norm_g.reshape(1, d), w, pos, freq)


def _outproj_body(h_ref, xa_ref, xb_ref, wa_ref, wb_ref, o_ref):
    o_ref[...] = (h_ref[...]
                  + jnp.dot(xa_ref[...].astype(BF16), wa_ref[...], preferred_element_type=F32)
                  + jnp.dot(xb_ref[...].astype(BF16), wb_ref[...], preferred_element_type=F32))


def _outproj(h, xa, xb, wa, wb):
    t, d = h.shape
    ka, kb = xa.shape[1], xb.shape[1]
    tm = _pick_tile(t, (512, 256, 128, 64, 32, 16, 8))
    return pl.pallas_call(
        _outproj_body,
        grid=(t // tm,),
        in_specs=[
            pl.BlockSpec((tm, d), lambda i: (i, 0)),
            pl.BlockSpec((tm, ka), lambda i: (i, 0)),
            pl.BlockSpec((tm, kb), lambda i: (i, 0)),
            pl.BlockSpec((ka, d), lambda i: (0, 0)),
            pl.BlockSpec((kb, d), lambda i: (0, 0)),
        ],
        out_specs=pl.BlockSpec((tm, d), lambda i: (i, 0)),
        out_shape=jax.ShapeDtypeStruct((t, d), F32),
        compiler_params=_cparams("parallel"),
        name="mix_outproj",
    )(h, xa, xb, wa, wb)


def _gdn_body(qkv_ref, z_ref, ab_ref, cst_ref, s0_ref, cw_ref, alog_ref, dtb_ref, og_ref, bd_ref, ex_ref,
              o_ref, cnew_ref, snew_ref,
              xbuf, s_scr, c_s, q_s, k_s, v_s, b_s, g_s,
              *, n_sub, tb, chunk, valid, n_heads, group_chunks):
    blk = pl.program_id(1)
    last_blk = pl.num_programs(1) - 1
    w_all = n_heads * HEAD_DIM
    lo = SUBLANES - (CONV_W - 1)
    c = chunk

    @pl.when(blk == 0)
    def _():
        for s in range(n_sub):
            xbuf[s, lo:SUBLANES, :] = cst_ref[s]
        s_scr[...] = s0_ref[...]

    cw = cw_ref[...]
    for s in range(n_sub):
        xbuf[s, SUBLANES:SUBLANES + tb, :] = qkv_ref[s * tb:(s + 1) * tb, :]
        y = xbuf[s, lo:lo + tb, :] * cw[0:1]
        for j in range(1, CONV_W):
            y = y + xbuf[s, lo + j:lo + j + tb, :] * cw[j:j + 1]
        c_s[s * tb:(s + 1) * tb, :] = y * _sigmoid(y)
        xbuf[s, lo:SUBLANES, :] = xbuf[s, lo + valid:SUBLANES + valid, :]

    @pl.when(blk == last_blk)
    def _():
        for s in range(n_sub):
            cnew_ref[s] = xbuf[s, lo:SUBLANES, :]

    bd = bd_ref[...]
    q = c_s[:, 0:w_all]
    k = c_s[:, w_all:2 * w_all]
    q_s[...] = q * lax.rsqrt(_dot_split(q * q, bd) + L2_EPS) * ATTN_SCALE
    k_s[...] = k * lax.rsqrt(_dot_split(k * k, bd) + L2_EPS)
    v_s[...] = c_s[:, 2 * w_all:3 * w_all]

    ab = ab_ref[...]
    g = -jnp.exp(alog_ref[...]) * _softplus(ab + dtb_ref[...])
    beta = _sigmoid(ab)
    if valid < tb:
        live = (_iota2(ab.shape, 0) & (tb - 1)) < valid
        g = jnp.where(live, g, 0.0)
        beta = jnp.where(live, beta, 0.0)
    g_s[...] = g
    b_s[...] = _dot_exact(beta, ex_ref[...])

    row = _iota2((c, c), 0)
    col = _iota2((c, c), 1)
    incl = row >= col
    strict = row > col
    og = og_ref[...]

    def run_group(base, chunk_subs):
        g_rows = len(chunk_subs) * c
        gc_all = _dot_exact(_tril_block_diag(g_rows, c), g_s[pl.ds(base, g_rows), :])
        gct_all = gc_all.T
        chains = []
        for j, sub in enumerate(chunk_subs):
            rs = pl.ds(base + j * c, c)
            js = slice(j * c, (j + 1) * c)
            for pair in range(n_heads // 2):
                lanes = slice(pair * LANES, (pair + 1) * LANES)
                q2, k2, v2, b2 = q_s[rs, lanes], k_s[rs, lanes], v_s[rs, lanes], b_s[rs, lanes]
                for half in range(2):
                    h = 2 * pair + half
                    hs = slice(half * HEAD_DIM, (half + 1) * HEAD_DIM)
                    ch = types.SimpleNamespace(j=j, sub=sub, h=h, pair=pair, half=half)
                    ch.q, ch.k, ch.v = q2[:, hs], k2[:, hs], v2[:, hs]
                    bh = b2[:, hs]
                    ch.gcol = gc_all[js, h:h + 1]
                    ch.glast = gc_all[(j + 1) * c - 1:(j + 1) * c, h:h + 1]
                    grow = gct_all[h:h + 1, js]
                    ch.decay = jnp.where(incl, jnp.exp(jnp.where(incl, ch.gcol - grow, 0.0)), 0.0)
                    ch.eg = jnp.exp(ch.gcol)
                    ch.kb = ch.k * bh
                    ch.vb = ch.v * bh
                    chains.append(ch)
        prods = [_dot_nt(jnp.concatenate([ch.kb, ch.q], axis=0), ch.k) for ch in chains]
        for ch, pr in zip(chains, prods):
            ch.n = -jnp.where(strict, pr[:c] * ch.decay, 0.0)
            ch.qk = jnp.where(incl, pr[c:] * ch.decay, 0.0)
        t_invs = _inv_i_minus_many([ch.n for ch in chains], c)
        sols = [_dot(t, jnp.concatenate([ch.vb, ch.kb * ch.eg], axis=1)) for t, ch in zip(t_invs, chains)]
        for ch, sol in zip(chains, sols):
            ch.u, ch.w = sol[:, :HEAD_DIM], sol[:, HEAD_DIM:]

        state = {}
        for wave in _waves(chunk_subs):
            cur = [ch for ch in chains if ch.j in wave]
            for ch in cur:
                if (ch.sub, ch.h) not in state:
                    state[(ch.sub, ch.h)] = s_scr[ch.sub, ch.h]
            ws_qs = [_dot(jnp.concatenate([ch.w, ch.q * ch.eg], axis=0), state[(ch.sub, ch.h)]) for ch in cur]
            v_news = [ch.u - x[:c] for ch, x in zip(cur, ws_qs)]
            o_parts = [_dot(ch.qk, vn) for ch, vn in zip(cur, v_news)]
            s_parts = [_dot_tn(ch.k * jnp.exp(ch.glast - ch.gcol), vn) for ch, vn in zip(cur, v_news)]
            for ch, x, op, sp in zip(cur, ws_qs, o_parts, s_parts):
                ch.o = _rms(x[c:] + op) * og
                state[(ch.sub, ch.h)] = state[(ch.sub, ch.h)] * jnp.exp(ch.glast) + sp
        for (sub, h), val in state.items():
            s_scr[sub, h] = val

        by_key = {(ch.j, ch.h): ch for ch in chains}
        for j in range(len(chunk_subs)):
            rs = pl.ds(base + j * c, c)
            for pair in range(n_heads // 2):
                lanes = slice(pair * LANES, (pair + 1) * LANES)
                z2 = z_ref[rs, lanes]
                o2 = jnp.concatenate([by_key[(j, 2 * pair)].o, by_key[(j, 2 * pair + 1)].o], axis=1)
                o_ref[rs, lanes] = o2 * (z2 * _sigmoid(z2))

    chunks_per_seq = tb // c
    if n_sub > 1 or chunks_per_seq <= group_chunks:
        run_group(0, [s for s in range(n_sub) for _ in range(chunks_per_seq)])
    else:
        g_rows = group_chunks * c

        def step(it, carry):
            run_group(pl.multiple_of(it * g_rows, g_rows), [0] * group_chunks)
            return carry

        lax.fori_loop(0, chunks_per_seq // group_chunks, step, 0)

    @pl.when(blk == last_blk)
    def _():
        snew_ref[...] = s_scr[...]


def _gdn(qkv, z, ab, conv_state, s0, conv_w, a_log, dt_bias, out_g, *, seq_rows, valid, chunk, tb, n_sub,
         group_chunks):
    n_seq, n_heads = s0.shape[0], s0.shape[1]
    w_all = n_heads * HEAD_DIM
    n_blk = seq_rows // tb
    assert n_sub == 1 or n_blk == 1
    assert n_seq % n_sub == 0 and (tb // chunk) % group_chunks == 0 or tb // chunk <= group_chunks
    _log2(tb), _log2(chunk)
    lane_h = np.arange(w_all) // HEAD_DIM
    bd = jnp.asarray(lane_h[:, None] == lane_h[None, :], BF16)
    ex = jnp.asarray((np.arange(LANES)[:, None] - n_heads) == lane_h[None, :], F32)
    row128 = lambda x: jnp.zeros((1, LANES), F32).at[0, :n_heads].set(x)
    rows_blk = n_sub * tb
    tok = lambda width: pl.BlockSpec((rows_blk, width), lambda s, b: (s * n_blk + b, 0))
    full = lambda a: pl.BlockSpec(a.shape, lambda s, b: (0,) * a.ndim)
    params = (conv_w, row128(a_log), row128(dt_bias), out_g.reshape(1, HEAD_DIM), bd, ex)
    rows = n_seq * seq_rows
    return pl.pallas_call(
        functools.partial(_gdn_body, n_sub=n_sub, tb=tb, chunk=chunk, valid=valid, n_heads=n_heads,
                          group_chunks=group_chunks),
        grid=(n_seq // n_sub, n_blk),
        in_specs=[tok(3 * w_all), tok(w_all), tok(LANES),
                  pl.BlockSpec((n_sub, CONV_W - 1, 3 * w_all), lambda s, b: (s, 0, 0)),
                  pl.BlockSpec((n_sub, n_heads, HEAD_DIM, HEAD_DIM), lambda s, b: (s, 0, 0, 0))]
        + [full(a) for a in params],
        out_specs=[tok(w_all),
                   pl.BlockSpec((n_sub, CONV_W - 1, 3 * w_all), lambda s, b: (s, 0, 0)),
                   pl.BlockSpec((n_sub, n_heads, HEAD_DIM, HEAD_DIM), lambda s, b: (s, 0, 0, 0))],
        out_shape=[jax.ShapeDtypeStruct((rows, w_all), F32),
                   jax.ShapeDtypeStruct(conv_state.shape, F32),
                   jax.ShapeDtypeStruct(s0.shape, F32)],
        scratch_shapes=[pltpu.VMEM((n_sub, SUBLANES + tb, 3 * w_all), F32),
                        pltpu.VMEM((n_sub, n_heads, HEAD_DIM, HEAD_DIM), F32),
                        pltpu.VMEM((rows_blk, 3 * w_all), F32),
                        pltpu.VMEM((rows_blk, w_all), F32), pltpu.VMEM((rows_blk, w_all), F32),
                        pltpu.VMEM((rows_blk, w_all), F32), pltpu.VMEM((rows_blk, w_all), F32),
                        pltpu.VMEM((rows_blk, LANES), F32)],
        compiler_params=_cparams("parallel", "arbitrary"),
        name="gdn_mixer",
    )(qkv, z, ab, conv_state, s0, *params)


def _rwkv_body(p_ref, sh_ref, s0_ref, mu_ref, w0_ref, a0_ref, kk_ref, ka_ref, rk_ref, lnw_ref, lnb_ref,
               w2_ref, a2_ref, g2_ref, bd_ref,
               o_ref, shnew_ref, snew_ref,
               xbuf, s_scr, r_s, k_s, v_s, kk_s, a_s, ld_s, gate_s, bonus_s,
               *, n_sub, tb, chunk, valid, n_heads, group_chunks):
    blk = pl.program_id(1)
    last_blk = pl.num_programs(1) - 1
    w_all = n_heads * HEAD_DIM
    c = chunk

    @pl.when(blk == 0)
    def _():
        for s in range(n_sub):
            xbuf[s, SUBLANES - 1:SUBLANES, :] = sh_ref[s]
        s_scr[...] = s0_ref[...]

    shifted = []
    for s in range(n_sub):
        xbuf[s, SUBLANES:SUBLANES + tb, :] = p_ref[s * tb:(s + 1) * tb, :]
        shifted.append(xbuf[s, SUBLANES - 1:SUBLANES - 1 + tb, :])
        xbuf[s, SUBLANES - 1:SUBLANES, :] = xbuf[s, SUBLANES - 1 + valid:SUBLANES + valid, :]
    shifted = shifted[0] if n_sub == 1 else jnp.concatenate(shifted, axis=0)

    @pl.when(blk == last_blk)
    def _():
        for s in range(n_sub):
            shnew_ref[s] = xbuf[s, SUBLANES - 1:SUBLANES, :]

    p = p_ref[...]
    xs = p + (shifted - p) * mu_ref[...]
    r = xs[:, 0:w_all]
    k = xs[:, w_all:2 * w_all]
    v = xs[:, 2 * w_all:3 * w_all]
    lo = xs[:, 3 * w_all:3 * w_all + LANES]
    g_lo = xs[:, 3 * w_all + LANES:]
    w_log = -_softplus(-(w0_ref[...] + _dot(jnp.tanh(lo), w2_ref[...]))) - 0.5
    log_decay = -jnp.exp(w_log)
    a = _sigmoid(a0_ref[...] + _dot(lo, a2_ref[...]))
    gate_s[...] = _dot(_sigmoid(g_lo), g2_ref[...])
    bd = bd_ref[...]
    kkx = k * kk_ref[...]
    kk = kkx * lax.rsqrt(_dot_split(kkx * kkx, bd) + L2_EPS)
    k = k * (1.0 + (a - 1.0) * ka_ref[...])
    bonus_s[...] = _dot_split(r * k * rk_ref[...], bd) * v
    if valid < tb:
        live = (_iota2(r.shape, 0) & (tb - 1)) < valid
        log_decay = jnp.where(live, log_decay, 0.0)
        kk = jnp.where(live, kk, 0.0)
        v = jnp.where(live, v, 0.0)
    r_s[...] = r
    k_s[...] = k
    v_s[...] = v
    kk_s[...] = kk
    a_s[...] = a
    ld_s[...] = log_decay

    row = _iota2((c, c), 0)
    col = _iota2((c, c), 1)
    incl = row >= col
    strict = row > col

    def run_group(base, chunk_subs):
        g_rows = len(chunk_subs) * c
        rows_g = pl.ds(base, g_rows)
        ld = ld_s[rows_g, :]
        gcum = _dot_exact(_tril_block_diag(g_rows, c), ld)
        e_pos = jnp.exp(gcum)
        e_neg = jnp.exp(-gcum)
        kk_g = kk_s[rows_g, :]
        r_t = r_s[rows_g, :] * e_pos
        b_t = kk_g * jnp.exp(gcum - ld)
        a_t = -(kk_g * a_s[rows_g, :]) * e_neg
        k_t = k_s[rows_g, :] * e_neg
        v_g = v_s[rows_g, :]
        chains = []
        for j, sub in enumerate(chunk_subs):
            js = slice(j * c, (j + 1) * c)
            e_last = e_pos[(j + 1) * c - 1:(j + 1) * c, :]
            for h in range(n_heads):
                hs = slice(h * HEAD_DIM, (h + 1) * HEAD_DIM)
                ch = types.SimpleNamespace(j=j, sub=sub, h=h)
                ch.b, ch.r, ch.v = b_t[js, hs], r_t[js, hs], v_g[js, hs]
                ch.left = jnp.concatenate([ch.b, ch.r], axis=0)
                ch.right = jnp.concatenate([a_t[js, hs], k_t[js, hs]], axis=0)
                ch.e_last = e_last[:, hs]
                chains.append(ch)
        ms = [_dot_nt(ch.left, ch.right) for ch in chains]
        for ch, m in zip(chains, ms):
            ch.a_ab = jnp.where(strict, m[:c, :c], 0.0)
            ch.a_kb = jnp.where(strict, m[:c, c:], 0.0)
            ch.q_ak = jnp.concatenate([jnp.where(incl, m[c:, :c], 0.0), jnp.where(incl, m[c:, c:], 0.0)], axis=1)
        akvs = [_dot(ch.a_kb, ch.v) for ch in chains]
        t_invs = _inv_i_minus_many([ch.a_ab for ch in chains], c)
        tbs = [_dot(t, jnp.concatenate([ch.b, akv], axis=1)) for t, ch, akv in zip(t_invs, chains, akvs)]
        for ch, x in zip(chains, tbs):
            ch.tb_r = jnp.concatenate([x[:, :HEAD_DIM], ch.r], axis=0)
            ch.takv = x[:, HEAD_DIM:]

        state = {}
        for wave in _waves(chunk_subs):
            cur = [ch for ch in chains if ch.j in wave]
            for ch in cur:
                if (ch.sub, ch.h) not in state:
                    state[(ch.sub, ch.h)] = s_scr[ch.sub, ch.h]
            lss = [_dot_nt(ch.tb_r, state[(ch.sub, ch.h)]) for ch in cur]
            uvs = [jnp.concatenate([x[:c] + ch.takv, ch.v], axis=0) for ch, x in zip(cur, lss)]
            o_parts = [_dot(ch.q_ak, uv) for ch, uv in zip(cur, uvs)]
            s_parts = [_dot_tn(uv, ch.right * ch.e_last) for ch, uv in zip(cur, uvs)]
            for ch, x, op, sp in zip(cur, lss, o_parts, s_parts):
                o = x[c:] + op
                mean = jnp.mean(o, axis=-1, keepdims=True)
                var = jnp.mean(jnp.square(o - mean), axis=-1, keepdims=True)
                ch.o = (o - mean) * lax.rsqrt(var + RWKV_GN_EPS)
                state[(ch.sub, ch.h)] = state[(ch.sub, ch.h)] * ch.e_last + sp
        for (sub, h), val in state.items():
            s_scr[sub, h] = val

        by_key = {(ch.j, ch.h): ch for ch in chains}
        for j in range(len(chunk_subs)):
            rs = pl.ds(base + j * c, c)
            for pair in range(n_heads // 2):
                lanes = slice(pair * LANES, (pair + 1) * LANES)
                on = jnp.concatenate([by_key[(j, 2 * pair)].o, by_key[(j, 2 * pair + 1)].o], axis=1)
                on = on * lnw_ref[:, lanes] + lnb_ref[:, lanes]
                o_ref[rs, lanes] = (on + bonus_s[rs, lanes]) * gate_s[rs, lanes]

    chunks_per_seq = tb // c
    if n_sub > 1 or chunks_per_seq <= group_chunks:
        run_group(0, [s for s in range(n_sub) for _ in range(chunks_per_seq)])
    else:
        g_rows = group_chunks * c

        def step(it, carry):
            run_group(pl.multiple_of(it * g_rows, g_rows), [0] * group_chunks)
            return carry

        lax.fori_loop(0, chunks_per_seq // group_chunks, step, 0)

    @pl.when(blk == last_blk)
    def _():
        snew_ref[...] = s_scr[...]


def _rwkv(p, shift_state, s0, mu, w0, w2, a0, a2, g2, k_k, k_a, r_k, ln_w, ln_b, *, seq_rows, valid, chunk, tb,
          n_sub, group_chunks):
    n_seq, n_heads = s0.shape[0], s0.shape[1]
    w_all = n_heads * HEAD_DIM
    width = p.shape[1]
    n_blk = seq_rows // tb
    assert n_sub == 1 or n_blk == 1
    _log2(tb), _log2(chunk)
    lane_h = np.arange(w_all) // HEAD_DIM
    bd = jnp.asarray(lane_h[:, None] == lane_h[None, :], BF16)
    lora = w2.shape[0]
    w2p = jnp.zeros((LANES, w_all), BF16).at[:lora].set(w2.astype(BF16))
    a2p = jnp.zeros((LANES, w_all), BF16).at[lora:lora + a2.shape[0]].set(a2.astype(BF16))
    r1 = lambda x: x.reshape(1, -1)
    params = (r1(mu), r1(w0), r1(a0), r1(k_k), r1(k_a), r1(r_k), r1(ln_w), r1(ln_b),
              w2p, a2p, g2.astype(BF16), bd)
    rows_blk = n_sub * tb
    tok = lambda wd: pl.BlockSpec((rows_blk, wd), lambda s, b: (s * n_blk + b, 0))
    full = lambda arr: pl.BlockSpec(arr.shape, lambda s, b: (0,) * arr.ndim)
    rows = n_seq * seq_rows
    return pl.pallas_call(
        functools.partial(_rwkv_body, n_sub=n_sub, tb=tb, chunk=chunk, valid=valid, n_heads=n_heads,
                          group_chunks=group_chunks),
        grid=(n_seq // n_sub, n_blk),
        in_specs=[tok(width),
                  pl.BlockSpec((n_sub, 1, width), lambda s, b: (s, 0, 0)),
                  pl.BlockSpec((n_sub, n_heads, HEAD_DIM, HEAD_DIM), lambda s, b: (s, 0, 0, 0))]
        + [full(arr) for arr in params],
        out_specs=[tok(w_all),
                   pl.BlockSpec((n_sub, 1, width), lambda s, b: (s, 0, 0)),
                   pl.BlockSpec((n_sub, n_heads, HEAD_DIM, HEAD_DIM), lambda s, b: (s, 0, 0, 0))],
        out_shape=[jax.ShapeDtypeStruct((rows, w_all), F32),
                   jax.ShapeDtypeStruct(shift_state.shape, F32),
                   jax.ShapeDtypeStruct(s0.shape, F32)],
        scratch_shapes=[pltpu.VMEM((n_sub, SUBLANES + tb, width), F32),
                        pltpu.VMEM((n_sub, n_heads, HEAD_DIM, HEAD_DIM), F32)]
        + [pltpu.VMEM((rows_blk, w_all), F32) for _ in range(8)],
        compiler_params=_cparams("parallel", "arbitrary"),
        name="rwkv7_mixer",
    )(p, shift_state, s0, *params)


def _sb_prompt_body(q_ref, k_ref, v_ref, u_ref, o_ref, *, tq, tk, n_pairs):
    qi = pl.program_id(2)
    n_diag = tq // tk
    n_h = 2 * n_pairs
    lane = _iota2((tq, LANES), 1)
    q_heads = []
    for pair in range(n_pairs):
        q = q_ref[:, pair * LANES:(pair + 1) * LANES] * ATTN_SCALE
        q_heads += [jnp.where(lane < HEAD_DIM, q, 0.0).astype(BF16),
                    jnp.where(lane >= HEAD_DIM, q, 0.0).astype(BF16)]
    qrow = _iota2((tq, tk), 0)
    kidx = _iota2((tq, tk), 1)
    u = u_ref[...]

    def visit(k0, carry, mask):
        ks = [k_ref[pl.ds(k0, tk), pair * LANES:(pair + 1) * LANES].astype(BF16) for pair in range(n_pairs)]
        vs = [v_ref[pl.ds(k0, tk), pair * LANES:(pair + 1) * LANES].astype(BF16) for pair in range(n_pairs)]
        zs = [lax.dot_general(q_heads[hh], ks[hh // 2], (((1,), (1,)), ((), ())), preferred_element_type=F32)
              for hh in range(n_h)]
        sps = [_softplus(z) for z in zs]
        lss = [-sp if mask is None else jnp.where(mask, -sp, 0.0) for sp in sps]
        cums = [_dot_split(ls, u) for ls in lss]
        ws = []
        for hh in range(n_h):
            w = jnp.exp(zs[hh] - sps[hh] + (carry[2 * hh] + cums[hh]))
            ws.append((w if mask is None else jnp.where(mask, w, 0.0)).astype(BF16))
        pvs = [jnp.dot(ws[hh], vs[hh // 2], preferred_element_type=F32) for hh in range(n_h)]
        new = []
        for hh in range(n_h):
            new += [carry[2 * hh] + jnp.sum(lss[hh], axis=-1, keepdims=True), carry[2 * hh + 1] + pvs[hh]]
        return tuple(new)

    carry = (jnp.zeros((tq, 1), F32), jnp.zeros((tq, LANES), F32)) * n_h
    for d in range(n_diag):
        off = (n_diag - 1 - d) * tk
        k0 = pl.multiple_of(qi * tq + off, tk)
        carry = visit(k0, carry, (kidx + off) < qrow)

    def alive(state):
        top = jnp.max(state[1])
        for hh in range(1, n_h):
            top = jnp.maximum(top, jnp.max(state[1 + 2 * hh]))
        return jnp.logical_and(state[0] < qi * n_diag, top > SB_DEAD_LOG)

    def body(state):
        jj = state[0]
        k0 = pl.multiple_of((qi * n_diag - 1 - jj) * tk, tk)
        return (jj + 1,) + visit(k0, state[1:], None)

    res = lax.while_loop(alive, body, (jnp.int32(0),) + carry)
    for pair in range(n_pairs):
        o_ref[:, pair * LANES:(pair + 1) * LANES] = jnp.where(lane < HEAD_DIM, res[2 + 4 * pair], res[4 + 4 * pair])


def _sb_prompt(p, n_batch, seq, col_q, col_k, col_v, n_heads):
    tq = _pick_tile(seq, (256, 128, 64, 32, 16, 8))
    tk = _pick_tile(tq, (128, 64, 32, 16, 8))
    n_q = seq // tq
    n_pairs = 2 if n_heads % 4 == 0 else 1
    wblk = n_pairs * LANES
    n_grp = n_heads * HEAD_DIM // wblk
    u = jnp.asarray(np.arange(tk)[:, None] > np.arange(tk)[None, :], BF16)
    return pl.pallas_call(
        functools.partial(_sb_prompt_body, tq=tq, tk=tk, n_pairs=n_pairs),
        grid=(n_batch, n_grp, n_q),
        in_specs=[
            pl.BlockSpec((tq, wblk), lambda b, g, i: (b * n_q + i, col_q // wblk + g)),
            pl.BlockSpec((seq, wblk), lambda b, g, i: (b, col_k // wblk + g)),
            pl.BlockSpec((seq, wblk), lambda b, g, i: (b, col_v // wblk + g)),
            pl.BlockSpec((tk, tk), lambda b, g, i: (0, 0)),
        ],
        out_specs=pl.BlockSpec((tq, wblk), lambda b, g, i: (b * n_q + i, g)),
        out_shape=jax.ShapeDtypeStruct((n_batch * seq, n_heads * HEAD_DIM), F32),
        compiler_params=_cparams("parallel", "parallel", "arbitrary"),
        name="sb_attn_prompt",
    )(p, p, p, u)


def _lambda_of(lam_ref, lam_init):
    lv = lam_ref[...]
    return (jnp.exp(jnp.sum(lv[0:1] * lv[1:2], axis=-1, keepdims=True))
            - jnp.exp(jnp.sum(lv[2:3] * lv[3:4], axis=-1, keepdims=True)) + lam_init)


def _diff_prompt_body(q_ref, k_ref, v_ref, lam_ref, sn_ref, o_ref, *, tq, tk, lam_init):
    qi = pl.program_id(2)
    lane = _iota2((tq, LANES), 1)
    q = q_ref[...] * ATTN_SCALE
    q_maps = [jnp.where(lane < HEAD_DIM, q, 0.0).astype(BF16),
              jnp.where(lane >= HEAD_DIM, q, 0.0).astype(BF16)]

    def visit(k0, carry, mask):
        k = k_ref[pl.ds(k0, tk), :].astype(BF16)
        v = v_ref[pl.ds(k0, tk), :].astype(BF16)
        ss = [lax.dot_general(qm, k, (((1,), (1,)), ((), ())), preferred_element_type=F32) for qm in q_maps]
        if mask is not None:
            ss = [jnp.where(mask, s, NEG_BIG) for s in ss]
        m_news = [jnp.maximum(carry[3 * mm], jnp.max(ss[mm], axis=-1, keepdims=True)) for mm in range(2)]
        prs = [jnp.exp(ss[mm] - m_news[mm]) for mm in range(2)]
        if mask is not None:
            prs = [jnp.where(mask, pr, 0.0) for pr in prs]
        pvs = [jnp.dot(pr.astype(BF16), v, preferred_element_type=F32) for pr in prs]
        new = []
        for mm in range(2):
            m_run, l_run, acc = carry[3 * mm:3 * mm + 3]
            corr = jnp.exp(m_run - m_news[mm])
            new += [m_news[mm], l_run * corr + jnp.sum(prs[mm], axis=-1, keepdims=True), acc * corr + pvs[mm]]
        return tuple(new)

    one = (jnp.full((tq, 1), NEG_BIG, F32), jnp.zeros((tq, 1), F32), jnp.zeros((tq, LANES), F32))
    n_full = (qi * tq) // tk
    res = lax.fori_loop(0, n_full, lambda kb, cr: visit(pl.multiple_of(kb * tk, tk), cr, None), one * 2)
    k0 = pl.multiple_of(n_full * tk, tk)
    mask = (k0 + _iota2((tq, tk), 1)) <= (qi * tq + _iota2((tq, tk), 0))
    res = visit(k0, res, mask)
    lam = _lambda_of(lam_ref, lam_init)
    o = res[2] / res[1] - lam * (res[5] / res[4])
    o_ref[...] = _rms(o) * sn_ref[...] * (1.0 - lam_init)


def _diff_prompt(p, n_batch, seq, col_q, col_k, col_v, n_heads, lam_vecs, sub_norm, lam_init):
    tq = _pick_tile(seq, (512, 256, 128, 64, 32, 16, 8))
    tk = _pick_tile(seq, (512, 256, 128, 64, 32, 16, 8))
    assert tk % tq == 0
    n_q = seq // tq
    col_q, col_k, col_v = col_q // LANES, col_k // LANES, col_v // LANES
    return pl.pallas_call(
        functools.partial(_diff_prompt_body, tq=tq, tk=tk, lam_init=lam_init),
        grid=(n_batch, n_heads, n_q),
        in_specs=[
            pl.BlockSpec((tq, LANES), lambda b, h, i: (b * n_q + i, col_q + h)),
            pl.BlockSpec((seq, LANES), lambda b, h, i: (b, col_k + h)),
            pl.BlockSpec((seq, LANES), lambda b, h, i: (b, col_v + h)),
            pl.BlockSpec(lam_vecs.shape, lambda b, h, i: (0, 0)),
            pl.BlockSpec((1, LANES), lambda b, h, i: (0, 0)),
        ],
        out_specs=pl.BlockSpec((tq, LANES), lambda b, h, i: (b * n_q + i, h)),
        out_shape=jax.ShapeDtypeStruct((n_batch * seq, n_heads * LANES), F32),
        compiler_params=_cparams("parallel", "parallel", "arbitrary"),
        name="diff_attn_prompt",
    )(p, p, p, lam_vecs, sub_norm.reshape(1, LANES))


def _block_diag_queries(q, n_groups):
    lane_g = _iota2(q.shape, 1) >> HEAD_SHIFT
    return jnp.concatenate([jnp.where(lane_g == g, q, 0.0) for g in range(n_groups)], axis=0)


def _sb_sample_body(pt_ref, q_ref, kn_ref, vn_ref, *rest, n_new, n_groups, n_pages):
    del pt_ref
    kt_refs, vt_refs = rest[:n_pages], rest[n_pages:2 * n_pages]
    u_ref, o_ref = rest[2 * n_pages:]
    rows = n_groups * SUBLANES
    width = n_groups * HEAD_DIM
    qrow = _iota2((rows, 1), 0) & (SUBLANES - 1)

    qbd = _block_diag_queries(q_ref[...] * ATTN_SCALE, n_groups)
    c = jnp.zeros((rows, 1), F32)
    acc = jnp.zeros((rows, width), F32)
    kn = kn_ref[...]
    vn = vn_ref[...]
    for t in range(n_new - 1, -1, -1):
        z = jnp.sum(qbd * kn[t:t + 1, :], axis=-1, keepdims=True)
        vis = t < qrow
        sp = _softplus(z)
        acc = acc + jnp.where(vis, jnp.exp(z - sp + c), 0.0) * vn[t:t + 1, :]
        c = c + jnp.where(vis, -sp, 0.0)

    page = lambda x, j: x[:, j * LANES:(j + 1) * LANES]
    kt_all = jnp.concatenate([r[...].reshape(width, LANES).astype(BF16) for r in kt_refs], axis=1)
    vt_all = jnp.concatenate([r[...].reshape(width, LANES).astype(BF16) for r in vt_refs], axis=1)
    z = jnp.dot(qbd.astype(BF16), kt_all, preferred_element_type=F32)
    sp = _softplus(z)
    sp_rows = jnp.concatenate([page(sp, j) for j in range(n_pages)], axis=0)
    cum_rows = _dot_split(sp_rows, u_ref[...])
    tot_rows = jnp.sum(sp_rows, axis=-1, keepdims=True)
    ws = [None] * n_pages
    for j in range(n_pages - 1, -1, -1):
        rj = slice(j * rows, (j + 1) * rows)
        ws[j] = jnp.exp(page(z, j) - page(sp, j) + (c - cum_rows[rj, :]))
        c = c - tot_rows[rj, :]
    acc = acc + lax.dot_general(jnp.concatenate(ws, axis=1).astype(BF16), vt_all, (((1,), (1,)), ((), ())),
                                preferred_element_type=F32)

    lane_g = _iota2((SUBLANES, width), 1) >> HEAD_SHIFT
    out = jnp.zeros((SUBLANES, width), F32)
    for g in range(n_groups):
        out = out + jnp.where(lane_g == g, acc[g * SUBLANES:(g + 1) * SUBLANES, :], 0.0)
    o_ref[...] = out


def _sb_sample(page_table, p3, col0, cache_kt, cache_vt, layer, n_new):
    n_b, n_pages = page_table.shape
    n_heads = cache_kt.shape[2]
    width = n_heads * HEAD_DIM
    page = cache_kt.shape[-1]
    u = jnp.asarray(np.arange(page)[:, None] > np.arange(page)[None, :], BF16)
    tok = lambda c: pl.BlockSpec((None, SUBLANES, width), lambda b, pt: (b, 0, c))
    pg = [pl.BlockSpec((None, None, n_heads, HEAD_DIM, page), lambda b, pt, j=j: (layer, pt[b, j], 0, 0, 0))
          for j in range(n_pages)]
    return pl.pallas_call(
        functools.partial(_sb_sample_body, n_new=n_new, n_groups=n_heads, n_pages=n_pages),
        grid_spec=pltpu.PrefetchScalarGridSpec(
            num_scalar_prefetch=1,
            grid=(n_b,),
            in_specs=[tok(col0), tok(col0 + 1), tok(col0 + 2)] + pg + pg
            + [pl.BlockSpec((page, page), lambda b, pt: (0, 0))],
            out_specs=tok(0)),
        out_shape=jax.ShapeDtypeStruct((n_b, SUBLANES, width), F32),
        compiler_params=_cparams("parallel"),
        name="sb_attn_sample",
    )(page_table, p3, p3, p3, *([cache_kt] * n_pages), *([cache_vt] * n_pages), u)


def _diff_sample_body(pt_ref, q_ref, kn_ref, vn_ref, *rest, n_new, n_heads, n_pages, lam_init):
    del pt_ref
    kt_refs, v_refs = rest[:n_pages], rest[n_pages:2 * n_pages]
    ex_ref, lam_ref, sn_ref, o_ref = rest[2 * n_pages:]
    n_groups = 2 * n_heads
    rows = n_groups * SUBLANES
    hrows = 2 * SUBLANES
    qrow = _iota2((rows, 1), 0) & (SUBLANES - 1)

    qbd = _block_diag_queries(q_ref[...] * ATTN_SCALE, n_groups)
    kn = kn_ref[...]
    vn = vn_ref[...]
    m_run = jnp.full((rows, 1), NEG_BIG, F32)
    l_run = jnp.zeros((rows, 1), F32)
    acc = jnp.zeros((rows, LANES), F32)
    for t in range(n_new):
        s = jnp.sum(qbd * kn[t:t + 1, :], axis=-1, keepdims=True)
        vis = t <= qrow
        m_new = jnp.maximum(m_run, jnp.where(vis, s, NEG_BIG))
        corr = jnp.exp(m_run - m_new)
        pr = jnp.where(vis, jnp.exp(s - m_new), 0.0)
        v_rows = jnp.concatenate(
            [jnp.broadcast_to(vn[t:t + 1, h * LANES:(h + 1) * LANES], (hrows, LANES)) for h in range(n_heads)],
            axis=0)
        l_run = l_run * corr + pr
        acc = acc * corr + pr * v_rows
        m_run = m_new

    kt_all = jnp.concatenate([r[...].reshape(n_groups * HEAD_DIM, LANES).astype(BF16) for r in kt_refs], axis=1)
    s = jnp.dot(qbd.astype(BF16), kt_all, preferred_element_type=F32)
    m_new = jnp.maximum(m_run, jnp.max(s, axis=-1, keepdims=True))
    corr = jnp.exp(m_run - m_new)
    pr = jnp.exp(s - m_new)
    l_run = l_run * corr + jnp.sum(pr, axis=-1, keepdims=True)
    pr_rows = jnp.concatenate([pr[:, j * LANES:(j + 1) * LANES] for j in range(n_pages)], axis=0)
    spread = jnp.dot(pr_rows.astype(BF16), ex_ref[...], preferred_element_type=F32)
    own = ((_iota2((rows, LANES * n_heads), 1) & (n_heads - 1))
           == (_iota2((rows, LANES * n_heads), 0) >> _log2(hrows)))
    pr_exp = jnp.concatenate(
        [jnp.where(own, spread[j * rows:(j + 1) * rows, :], 0.0).astype(BF16) for j in range(n_pages)], axis=1)
    v_all = jnp.concatenate([r[...].astype(BF16) for r in v_refs], axis=0)
    o_all = (acc * corr + jnp.dot(pr_exp, v_all, preferred_element_type=F32)) / l_run
    lam = _lambda_of(lam_ref, lam_init)
    outs = []
    for h in range(n_heads):
        o = o_all[h * hrows:h * hrows + SUBLANES, :] - lam * o_all[h * hrows + SUBLANES:(h + 1) * hrows, :]
        outs.append(_rms(o) * sn_ref[...] * (1.0 - lam_init))
    o_ref[...] = jnp.concatenate(outs, axis=1)


def _diff_sample(page_table, p3, col0, cache_kt, cache_v, layer, n_new, lam_vecs, sub_norm, lam_init):
    n_b, n_pages = page_table.shape
    n_heads = cache_kt.shape[2]
    page = cache_kt.shape[-1]
    width = n_heads * 2 * HEAD_DIM
    tok = lambda c: pl.BlockSpec((None, SUBLANES, width), lambda b, pt: (b, 0, c))
    kpg = [pl.BlockSpec((None, None, n_heads, 2, HEAD_DIM, page), lambda b, pt, j=j: (layer, pt[b, j], 0, 0, 0, 0))
           for j in range(n_pages)]
    vpg = [pl.BlockSpec((None, None, page * n_heads, LANES), lambda b, pt, j=j: (layer, pt[b, j], 0, 0))
           for j in range(n_pages)]
    _log2(n_heads)
    ex = jnp.asarray(np.arange(page)[:, None] == (np.arange(page * n_heads)[None, :] // n_heads), BF16)
    cache_v2 = cache_v.reshape(cache_v.shape[:2] + (page * n_heads, LANES))
    return pl.pallas_call(
        functools.partial(_diff_sample_body, n_new=n_new, n_heads=n_heads, n_pages=n_pages, lam_init=lam_init),
        grid_spec=pltpu.PrefetchScalarGridSpec(
            num_scalar_prefetch=1,
            grid=(n_b,),
            in_specs=[tok(col0), tok(col0 + 1), tok(col0 + 2)] + kpg + vpg
            + [pl.BlockSpec(ex.shape, lambda b, pt: (0, 0)),
               pl.BlockSpec(lam_vecs.shape, lambda b, pt: (0, 0)), pl.BlockSpec((1, LANES), lambda b, pt: (0, 0))],
            out_specs=tok(0)),
        out_shape=jax.ShapeDtypeStruct((n_b, SUBLANES, width), F32),
        compiler_params=_cparams("parallel"),
        name="diff_attn_sample",
    )(page_table, p3, p3, p3, *([cache_kt] * n_pages), *([cache_v2] * n_pages),
      ex, lam_vecs, sub_norm.reshape(1, LANES))


def _even_mix(hp, hs, norm_g, dims, states, prm):
    bp, lp, bs, ls, ls_pad = dims
    (w_in, w_out, conv_w, a_log, dt_bias, out_g, mu, w0, w2, a0, a2, g2, k_k, k_a, r_k, ln_w, ln_b) = prm
    conv_s, gdn_s, shift_s, rwkv_s = states
    n_heads = a_log.shape[0]
    w_gdn = n_heads * HEAD_DIM
    gdn_in = 4 * w_gdn + 2 * n_heads
    ab_w = jnp.zeros((w_in.shape[0], LANES), F32).at[:, :2 * n_heads].set(w_in[:, 4 * w_gdn:gdn_in])
    w_cat = jnp.concatenate([w_in[:, :4 * w_gdn], w_in[:, gdn_in:], ab_w], axis=1).astype(BF16)
    rwkv_in = w_in.shape[1] - gdn_in
    splits = (3 * w_gdn, w_gdn, rwkv_in, LANES)
    w_out_b = w_out.astype(BF16)
    gdn_prm = (conv_w, a_log, dt_bias, out_g)
    rwkv_prm = (mu, w0, w2, a0, a2, g2, k_k, k_a, r_k, ln_w, ln_b)

    chunk_p = min(64, lp)
    tb_p = _pick_tile(lp, (256, 128, 64)) if lp >= 64 else lp
    cfg_p = dict(seq_rows=lp, valid=tb_p, chunk=chunk_p, tb=tb_p, n_sub=1, group_chunks=2)
    zeros = lambda shape: jnp.zeros(shape, F32)
    qkv, z, p_rwkv, ab = _proj_even(hp, norm_g, w_cat, splits)
    o_a, conv_p, gdn_p = _gdn(qkv, z, ab, zeros((bp,) + conv_s.shape[1:]), zeros((bp,) + gdn_s.shape[1:]),
                              *gdn_prm, **cfg_p)
    o_b, shift_p, rwkv_p = _rwkv(p_rwkv, zeros((bp,) + shift_s.shape[1:]), zeros((bp,) + rwkv_s.shape[1:]),
                                 *rwkv_prm, **cfg_p)
    hp = _outproj(hp, o_a, o_b, w_out_b[:w_gdn], w_out_b[w_gdn:])

    cfg_s = dict(seq_rows=ls_pad, valid=ls, chunk=ls_pad, tb=ls_pad, n_sub=_pick_tile(bs, (4, 2, 1)),
                 group_chunks=1)
    qkv, z, p_rwkv, ab = _proj_even(hs, norm_g, w_cat, splits)
    o_a, conv_n, gdn_n = _gdn(qkv, z, ab, conv_s, gdn_s, *gdn_prm, **cfg_s)
    o_b, shift_n, rwkv_n = _rwkv(p_rwkv, shift_s, rwkv_s, *rwkv_prm, **cfg_s)
    hs = _outproj(hs, o_a, o_b, w_out_b[:w_gdn], w_out_b[w_gdn:])
    return hp, hs, (conv_p, gdn_p, shift_p, rwkv_p), (conv_n, gdn_n, shift_n, rwkv_n)


def _odd_mix(hp, hs, norm_g, dims, caches, page_table, layer_idx, pos_p, pos_s, freq, prm, lam_init):
    bp, lp, bs, ls, ls_pad = dims
    assert ls_pad == SUBLANES
    w_in, w_out, lam_vecs, sub_norm = prm
    ck_sb, cv_sb, ck_d, cv_d = caches
    h_sb = ck_sb.shape[3]
    h_d = ck_d.shape[3]
    w_sb = h_sb * HEAD_DIM
    w_d = h_d * 2 * HEAD_DIM
    assert w_sb == w_d
    d0 = 3 * w_sb
    w_in_b = w_in.astype(BF16)
    w_out_b = w_out.astype(BF16)

    pp = _proj_odd(hp, norm_g, w_in_b, pos_p, freq, d0, d0 + 2 * w_d)
    o_sb = _sb_prompt(pp, bp, lp, 0, w_sb, 2 * w_sb, h_sb)
    o_d = _diff_prompt(pp, bp, lp, d0, d0 + w_d, d0 + 2 * w_d, h_d, lam_vecs, sub_norm, lam_init)
    hp = _outproj(hp, o_sb, o_d, w_out_b[:w_sb], w_out_b[w_sb:])

    ps = _proj_odd(hs, norm_g, w_in_b, pos_s, freq, d0, d0 + 2 * w_d)
    ps3 = ps.reshape(bs, ls_pad, ps.shape[-1])
    o_sb = _sb_sample(page_table, ps3, 0, jnp.transpose(ck_sb, (0, 1, 3, 4, 2)),
                      jnp.transpose(cv_sb, (0, 1, 3, 4, 2)), layer_idx, ls)
    o_d = _diff_sample(page_table, ps3, d0 // w_d, jnp.transpose(ck_d, (0, 1, 3, 4, 5, 2)), cv_d, layer_idx, ls,
                       lam_vecs, sub_norm, lam_init)
    hs = _outproj(hs, o_sb.reshape(bs * ls_pad, w_sb), o_d.reshape(bs * ls_pad, w_d),
                  w_out_b[:w_sb], w_out_b[w_sb:])

    def split(rows):
        b, l = rows.shape[:2]
        k_sb = rows[..., w_sb:2 * w_sb].reshape(b, l, h_sb, HEAD_DIM)
        v_sb = rows[..., 2 * w_sb:3 * w_sb].reshape(b, l, h_sb, HEAD_DIM)
        k_d = rows[..., d0 + w_d:d0 + 2 * w_d].reshape(b, l, h_d, 2, HEAD_DIM)
        v_d = rows[..., d0 + 2 * w_d:d0 + 3 * w_d].reshape(b, l, h_d, 2 * HEAD_DIM)
        return k_sb, v_sb, k_d, v_d

    return hp, hs, split(pp.reshape(bp, lp, pp.shape[-1])), split(ps3[:, :ls])


def kernel(x_prompt, x_sample, cache_k_sb, cache_v_sb, cache_k_diff, cache_v_diff, page_table, state_gdn, state_gdn_conv, state_rwkv, state_rwkv_shift, ffn_norm, ffn_w_gate, ffn_w_up, ffn_w_down, mix_norm, final_norm, ev_w_in, ev_w_out, gdn_conv_w, gdn_a_log, gdn_dt_bias, gdn_out_norm, rwkv_mu, rwkv_w0, rwkv_w2, rwkv_a0, rwkv_a2, rwkv_g2, rwkv_k_k, rwkv_k_a, rwkv_r_k, rwkv_ln_w, rwkv_ln_b, od_w_in, od_w_out, diff_lambda, diff_norm):
    bp, lp, d = x_prompt.shape
    bs, ls, _ = x_sample.shape
    depth = ffn_norm.shape[0]
    ls_pad = -(-ls // SUBLANES) * SUBLANES
    dims = (bp, lp, bs, ls, ls_pad)
    past_len = page_table.shape[1] * cache_k_sb.shape[2]

    hp = x_prompt.reshape(bp * lp, d)
    hs = jnp.pad(x_sample, ((0, 0), (0, ls_pad - ls), (0, 0))).reshape(bs * ls_pad, d)
    pos_p = jnp.tile(jnp.arange(lp), bp).astype(F32).reshape(-1, 1)
    pos_s = jnp.tile(past_len + jnp.arange(ls_pad), bs).astype(F32).reshape(-1, 1)
    half = ROPE_DIM // 2
    inv_freq = ROPE_THETA ** (-jnp.arange(half, dtype=F32) * 2.0 / ROPE_DIM)
    lane = np.arange(LANES) % HEAD_DIM
    freq = jnp.where(lane < ROPE_DIM, inv_freq[lane % half], 0.0).astype(F32).reshape(1, LANES)

    wg, wu, wd = (w.astype(BF16) for w in (ffn_w_gate, ffn_w_up, ffn_w_down))
    names = ("k_sb", "v_sb", "k_diff", "v_diff", "gdn_conv", "gdn", "rwkv_shift", "rwkv")
    new_p = {n: [] for n in names}
    new_s = {n: [] for n in names}
    for l in range(depth):
        i = l // 2
        hp = _ffn_half_step(hp, ffn_norm[l, 0], wg[l, 0], wu[l, 0], wd[l, 0])
        hs = _ffn_half_step(hs, ffn_norm[l, 0], wg[l, 0], wu[l, 0], wd[l, 0])
        if l % 2 == 0:
            prm = (ev_w_in[i], ev_w_out[i], gdn_conv_w[i], gdn_a_log[i], gdn_dt_bias[i], gdn_out_norm[i],
                   rwkv_mu[i], rwkv_w0[i], rwkv_w2[i], rwkv_a0[i], rwkv_a2[i], rwkv_g2[i],
                   rwkv_k_k[i], rwkv_k_a[i], rwkv_r_k[i], rwkv_ln_w[i], rwkv_ln_b[i])
            states = (state_gdn_conv[i], state_gdn[i], state_rwkv_shift[i], state_rwkv[i])
            hp, hs, st_p, st_s = _even_mix(hp, hs, mix_norm[l], dims, states, prm)
            for n, a_p, a_s in zip(("gdn_conv", "gdn", "rwkv_shift", "rwkv"), st_p, st_s):
                new_p[n].append(a_p)
                new_s[n].append(a_s)
        else:
            lam_init = 0.8 - 0.6 * math.exp(-0.3 * l)
            prm = (od_w_in[i], od_w_out[i], diff_lambda[i], diff_norm[i])
            caches = (cache_k_sb, cache_v_sb, cache_k_diff, cache_v_diff)
            hp, hs, kv_p, kv_s = _odd_mix(hp, hs, mix_norm[l], dims, caches, page_table, i, pos_p, pos_s, freq, prm,
                                          lam_init)
            for n, a_p, a_s in zip(("k_sb", "v_sb", "k_diff", "v_diff"), kv_p, kv_s):
                new_p[n].append(a_p)
                new_s[n].append(a_s)
        final = final_norm if l == depth - 1 else None
        hp = _ffn_half_step(hp, ffn_norm[l, 1], wg[l, 1], wu[l, 1], wd[l, 1], final)
        hs = _ffn_half_step(hs, ffn_norm[l, 1], wg[l, 1], wu[l, 1], wd[l, 1], final)

    y_prompt = hp.reshape(bp, lp, d)
    y_sample = hs.reshape(bs, ls_pad, d)[:, :ls]
    out = [y_prompt, y_sample]
    for n in ("k_sb", "v_sb", "k_diff", "v_diff", "gdn", "gdn_conv", "rwkv", "rwkv_shift"):
        out += [jnp.stack(new_p[n]), jnp.stack(new_s[n])]
    return tuple(out)
```

```python
import functools
import math
import types

import jax
import jax.numpy as jnp
import numpy as np
from jax import lax
from jax.experimental import pallas as pl
from jax.experimental.pallas import tpu as pltpu

F32 = jnp.float32
BF16 = jnp.bfloat16

HEAD_DIM = 64
HEAD_SHIFT = 6
NORM_EPS = 1e-6
L2_EPS = 1e-6
RWKV_GN_EPS = 64e-5
ROPE_DIM = HEAD_DIM // 4
ROPE_THETA = 500000.0
CONV_W = 4
ATTN_SCALE = HEAD_DIM ** -0.5
NEG_BIG = -1e30
SB_DEAD_LOG = -100.0

V7X_VMEM_LIMIT_BYTES = 56 * 1024 * 1024
SUBLANES = 8
LANES = 128


def _cparams(*sem):
    return pltpu.CompilerParams(dimension_semantics=sem, vmem_limit_bytes=V7X_VMEM_LIMIT_BYTES)


def _sigmoid(x):
    return 1.0 / (1.0 + jnp.exp(-x))


def _softplus(x):
    return jnp.maximum(x, 0.0) + jnp.log(1.0 + jnp.exp(-jnp.abs(x)))


def _rms(x, eps=NORM_EPS):
    return x * lax.rsqrt(jnp.mean(x * x, axis=-1, keepdims=True) + eps)


def _dot(a, b):
    return jnp.dot(a.astype(BF16), b.astype(BF16), preferred_element_type=F32)


def _dot_nt(a, b):
    return lax.dot_general(a.astype(BF16), b.astype(BF16), (((1,), (1,)), ((), ())),
                           preferred_element_type=F32)


def _dot_tn(a, b):
    return lax.dot_general(a.astype(BF16), b.astype(BF16), (((0,), (0,)), ((), ())),
                           preferred_element_type=F32)


def _dot_exact(a, b):
    return jnp.dot(a, b, precision=lax.Precision.HIGHEST, preferred_element_type=F32)


def _dot_split(x, m_bf16):
    hi = x.astype(BF16)
    lo = (x - hi.astype(F32)).astype(BF16)
    return (jnp.dot(hi, m_bf16, preferred_element_type=F32)
            + jnp.dot(lo, m_bf16, preferred_element_type=F32))


def _iota2(shape, dim):
    return lax.broadcasted_iota(jnp.int32, shape, dim)


def _log2(n):
    assert n > 0 and n & (n - 1) == 0, n
    return n.bit_length() - 1


def _inv_i_minus_many(n_mats, order):
    size = n_mats[0].shape[0]
    eye = (_iota2((size, size), 0) == _iota2((size, size), 1)).astype(F32)
    ts = [eye + n for n in n_mats]
    ps = list(n_mats)
    k = 1
    squared = False
    while 2 * k < order:
        if not squared:
            ps = [_dot(p, p) for p in ps]
            squared = True
        else:
            prods = [_dot(jnp.concatenate([p, t], axis=0), p) for p, t in zip(ps, ts)]
            ts = [t + pr[size:] for t, pr in zip(ts, prods)]
            ps = [pr[:size] for pr in prods]
        k *= 2
    if squared:
        ts = [t + _dot(t, p) for t, p in zip(ts, ps)]
    return ts


def _tril_block_diag(rows, chunk):
    sh = _log2(chunk)
    r = _iota2((rows, rows), 0)
    c = _iota2((rows, rows), 1)
    return jnp.where((r >> sh) == (c >> sh), jnp.where(r >= c, 1.0, 0.0), 0.0)


def _pair_rows(x2):
    first = _iota2(x2.shape, 1) < HEAD_DIM
    return jnp.concatenate([jnp.where(first, x2, 0.0), jnp.where(first, 0.0, x2)], axis=0)


def _pair_state(s_h0, s_h1):
    zero = jnp.zeros_like(s_h0)
    return jnp.concatenate([jnp.concatenate([s_h0, zero], axis=1), jnp.concatenate([zero, s_h1], axis=1)], axis=0)


def _pair_col(x_h0, x_h1, rows):
    return jnp.concatenate([jnp.broadcast_to(x_h0, (rows, 1)), jnp.broadcast_to(x_h1, (rows, 1))], axis=0)


def _head_halves_mean(x):
    first = _iota2(x.shape, 1) < HEAD_DIM
    m0 = jnp.sum(jnp.where(first, x, 0.0), axis=-1, keepdims=True)
    m1 = jnp.sum(jnp.where(first, 0.0, x), axis=-1, keepdims=True)
    return jnp.where(first, m0, m1) * (1.0 / HEAD_DIM)


def _waves(chunk_subs):
    by_sub = {}
    for j, sub in enumerate(chunk_subs):
        by_sub.setdefault(sub, []).append(j)
    depth = max(len(v) for v in by_sub.values())
    return [[v[w] for v in by_sub.values() if len(v) > w] for w in range(depth)]


def _ffn_body(h_ref, g_ref, wg_ref, wu_ref, wd_ref, fg_ref, o_ref, xn_ref, acc_ref, *, apply_final):
    j = pl.program_id(1)

    @pl.when(j == 0)
    def _():
        xn_ref[...] = (_rms(h_ref[...]) * g_ref[...]).astype(BF16)
        acc_ref[...] = jnp.zeros_like(acc_ref)

    xn = xn_ref[...]
    gate = jnp.dot(xn, wg_ref[...], preferred_element_type=F32)
    up = jnp.dot(xn, wu_ref[...], preferred_element_type=F32)
    act = (gate * _sigmoid(gate)) * up
    acc_ref[...] += jnp.dot(act.astype(BF16), wd_ref[...], preferred_element_type=F32)

    @pl.when(j == pl.num_programs(1) - 1)
    def _():
        out = h_ref[...] + 0.5 * acc_ref[...]
        if apply_final:
            out = _rms(out) * fg_ref[...]
        o_ref[...] = out


def _pick_tile(n, candidates):
    for c in candidates:
        if n % c == 0:
            return c
    return n


def _ffn_half_step(h, norm_g, wg, wu, wd, final_g=None):
    t, d = h.shape
    f = wg.shape[1]
    tm = _pick_tile(t, (512, 256, 128, 64, 32, 16, 8))
    tf = _pick_tile(f, (1408, 1024, 512, 256, 128))
    apply_final = final_g is not None
    fg = final_g if apply_final else norm_g
    return pl.pallas_call(
        functools.partial(_ffn_body, apply_final=apply_final),
        grid=(t // tm, f // tf),
        in_specs=[
            pl.BlockSpec((tm, d), lambda i, j: (i, 0)),
            pl.BlockSpec((1, d), lambda i, j: (0, 0)),
            pl.BlockSpec((d, tf), lambda i, j: (0, j)),
            pl.BlockSpec((d, tf), lambda i, j: (0, j)),
            pl.BlockSpec((tf, d), lambda i, j: (j, 0)),
            pl.BlockSpec((1, d), lambda i, j: (0, 0)),
        ],
        out_specs=pl.BlockSpec((tm, d), lambda i, j: (i, 0)),
        out_shape=jax.ShapeDtypeStruct((t, d), F32),
        scratch_shapes=[pltpu.VMEM((tm, d), BF16), pltpu.VMEM((tm, d), F32)],
        compiler_params=_cparams("parallel", "arbitrary"),
        name="ffn_half_step",
    )(h, norm_g.reshape(1, d), wg, wu, wd, fg.reshape(1, d))


def _proj_even_body(h_ref, g_ref, w_ref, *o_refs, splits):
    xn = (_rms(h_ref[...]) * g_ref[...]).astype(BF16)
    p = jnp.dot(xn, w_ref[...], preferred_element_type=F32)
    off = 0
    for o_ref, width in zip(o_refs, splits):
        o_ref[...] = p[:, off:off + width]
        off += width


def _proj_even(h, norm_g, w, splits):
    t, d = h.shape
    n = w.shape[1]
    tm = _pick_tile(t, (256, 128, 64, 32, 16, 8))
    return pl.pallas_call(
        functools.partial(_proj_even_body, splits=splits),
        grid=(t // tm,),
        in_specs=[
            pl.BlockSpec((tm, d), lambda i: (i, 0)),
            pl.BlockSpec((1, d), lambda i: (0, 0)),
            pl.BlockSpec((d, n), lambda i: (0, 0)),
        ],
        out_specs=[pl.BlockSpec((tm, s), lambda i: (i, 0)) for s in splits],
        out_shape=[jax.ShapeDtypeStruct((t, s), F32) for s in splits],
        compiler_params=_cparams("parallel"),
        name="proj_even",
    )(h, norm_g.reshape(1, d), w)


def _proj_odd_body(h_ref, g_ref, w_ref, pos_ref, freq_ref, o_ref, *, rope_lo, rope_hi):
    xn = (_rms(h_ref[...]) * g_ref[...]).astype(BF16)
    p = jnp.dot(xn, w_ref[...], preferred_element_type=F32)
    o_ref[...] = p
    ang = pos_ref[...] * freq_ref[...]
    cos, sin = jnp.cos(ang), jnp.sin(ang)
    lane = _iota2(ang.shape, 1) & (HEAD_DIM - 1)
    half = ROPE_DIM // 2
    sin_lo = jnp.where(lane < half, -sin, 0.0)
    sin_hi = jnp.where(lane < half, 0.0, jnp.where(lane < ROPE_DIM, sin, 0.0))
    for c0 in range(rope_lo, rope_hi, LANES):
        x = p[:, c0:c0 + LANES]
        up = pltpu.roll(x, LANES - half, axis=1)
        down = pltpu.roll(x, half, axis=1)
        o_ref[:, c0:c0 + LANES] = x * cos + up * sin_lo + down * sin_hi


def _proj_odd(h, norm_g, w, pos, freq, rope_lo, rope_hi):
    t, d = h.shape
    n = w.shape[1]
    tm = _pick_tile(t, (256, 128, 64, 32, 16, 8))
    return pl.pallas_call(
        functools.partial(_proj_odd_body, rope_lo=rope_lo, rope_hi=rope_hi),
        grid=(t // tm,),
        in_specs=[
            pl.BlockSpec((tm, d), lambda i: (i, 0)),
            pl.BlockSpec((1, d), lambda i: (0, 0)),
            pl.BlockSpec((d, n), lambda i: (0, 0)),
            pl.BlockSpec((tm, 1), lambda i: (i, 0)),
            pl.BlockSpec((1, LANES), lambda i: (0, 0)),
        ],
        out_specs=pl.BlockSpec((tm, n), lambda i: (i, 0)),
        out_shape=jax.ShapeDtypeStruct((t, n), F32),
        compiler_params=_cparams("parallel"),
        name="proj_odd",
    )(h, norm_g.reshape(1, d), w, pos, freq)


def _outproj_body(h_ref, xa_ref, xb_ref, wa_ref, wb_ref, o_ref):
    o_ref[...] = (h_ref[...]
                  + jnp.dot(xa_ref[...].astype(BF16), wa_ref[...], preferred_element_type=F32)
                  + jnp.dot(xb_ref[...].astype(BF16), wb_ref[...], preferred_element_type=F32))


def _outproj(h, xa, xb, wa, wb):
    t, d = h.shape
    ka, kb = xa.shape[1], xb.shape[1]
    tm = _pick_tile(t, (512, 256, 128, 64, 32, 16, 8))
    return pl.pallas_call(
        _outproj_body,
        grid=(t // tm,),
        in_specs=[
            pl.BlockSpec((tm, d), lambda i: (i, 0)),
            pl.BlockSpec((tm, ka), lambda i: (i, 0)),
            pl.BlockSpec((tm, kb), lambda i: (i, 0)),
            pl.BlockSpec((ka, d), lambda i: (0, 0)),
            pl.BlockSpec((kb, d), lambda i: (0, 0)),
        ],
        out_specs=pl.BlockSpec((tm, d), lambda i: (i, 0)),
        out_shape=jax.ShapeDtypeStruct((t, d), F32),
        compiler_params=_cparams("parallel"),
        name="mix_outproj",
    )(h, xa, xb, wa, wb)


def _gdn_body(qkv_ref, z_ref, ab_ref, cst_ref, s0_ref, cw_ref, alog_ref, dtb_ref, og_ref, bd_ref, ex_ref,
              o_ref, cnew_ref, snew_ref,
              xbuf, s_scr, c_s, q_s, k_s, v_s, b_s, g_s,
              *, n_sub, tb, chunk, valid, n_heads, group_chunks):
    blk = pl.program_id(1)
    last_blk = pl.num_programs(1) - 1
    w_all = n_heads * HEAD_DIM
    lo = SUBLANES - (CONV_W - 1)
    c = chunk

    @pl.when(blk == 0)
    def _():
        for s in range(n_sub):
            xbuf[s, lo:SUBLANES, :] = cst_ref[s]
            for pair in range(n_heads // 2):
                s_scr[s, pair] = _pair_state(s0_ref[s, 2 * pair], s0_ref[s, 2 * pair + 1])

    cw = cw_ref[...]
    for s in range(n_sub):
        xbuf[s, SUBLANES:SUBLANES + tb, :] = qkv_ref[s * tb:(s + 1) * tb, :]
        y = xbuf[s, lo:lo + tb, :] * cw[0:1]
        for j in range(1, CONV_W):
            y = y + xbuf[s, lo + j:lo + j + tb, :] * cw[j:j + 1]
        c_s[s * tb:(s + 1) * tb, :] = y * _sigmoid(y)
        xbuf[s, lo:SUBLANES, :] = xbuf[s, lo + valid:SUBLANES + valid, :]

    @pl.when(blk == last_blk)
    def _():
        for s in range(n_sub):
            cnew_ref[s] = xbuf[s, lo:SUBLANES, :]

    bd = bd_ref[...]
    q = c_s[:, 0:w_all]
    k = c_s[:, w_all:2 * w_all]
    q_s[...] = q * lax.rsqrt(_dot_split(q * q, bd) + L2_EPS) * ATTN_SCALE
    k_s[...] = k * lax.rsqrt(_dot_split(k * k, bd) + L2_EPS)
    v_s[...] = c_s[:, 2 * w_all:3 * w_all]

    ab = ab_ref[...]
    g = -jnp.exp(alog_ref[...]) * _softplus(ab + dtb_ref[...])
    beta = _sigmoid(ab)
    if valid < tb:
        live = (_iota2(ab.shape, 0) & (tb - 1)) < valid
        g = jnp.where(live, g, 0.0)
        beta = jnp.where(live, beta, 0.0)
    g_s[...] = g
    b_s[...] = _dot_exact(beta, ex_ref[...])

    c2 = 2 * c
    incl = _tril_block_diag(c2, c) > 0.5
    strict = jnp.logical_and(incl, _iota2((c2, c2), 0) != _iota2((c2, c2), 1))
    og2 = og_ref[...]

    def run_group(base, chunk_subs):
        g_rows = len(chunk_subs) * c
        gc_all = _dot_exact(_tril_block_diag(g_rows, c), g_s[pl.ds(base, g_rows), :])
        gct_all = gc_all.T
        chains = []
        for j, sub in enumerate(chunk_subs):
            rs = pl.ds(base + j * c, c)
            js = slice(j * c, (j + 1) * c)
            last = slice((j + 1) * c - 1, (j + 1) * c)
            for pair in range(n_heads // 2):
                lanes = slice(pair * LANES, (pair + 1) * LANES)
                h0, h1 = 2 * pair, 2 * pair + 1
                ch = types.SimpleNamespace(j=j, sub=sub, pair=pair)
                k2, b2 = k_s[rs, lanes], b_s[rs, lanes]
                ch.k = _pair_rows(k2)
                ch.kb = _pair_rows(k2 * b2)
                ch.q = _pair_rows(q_s[rs, lanes])
                ch.vb = _pair_rows(v_s[rs, lanes] * b2)
                gcol = jnp.concatenate([gc_all[js, h0:h0 + 1], gc_all[js, h1:h1 + 1]], axis=0)
                grow = jnp.concatenate([gct_all[h0:h0 + 1, js], gct_all[h1:h1 + 1, js]], axis=1)
                ch.decay = jnp.where(incl, jnp.exp(jnp.where(incl, gcol - grow, 0.0)), 0.0)
                ch.eg = jnp.exp(gcol)
                gl0, gl1 = gc_all[last, h0:h0 + 1], gc_all[last, h1:h1 + 1]
                ch.k_dec = ch.k * jnp.exp(_pair_col(gl0, gl1, c) - gcol)
                ch.s_scale = jnp.exp(_pair_col(gl0, gl1, HEAD_DIM))
                chains.append(ch)
        prods = [_dot_nt(jnp.concatenate([ch.kb, ch.q], axis=0), ch.k) for ch in chains]
        for ch, pr in zip(chains, prods):
            ch.n = -jnp.where(strict, pr[:c2] * ch.decay, 0.0)
            ch.qk = jnp.where(incl, pr[c2:] * ch.decay, 0.0)
        t_invs = _inv_i_minus_many([ch.n for ch in chains], c)
        sols = [_dot(t, jnp.concatenate([ch.vb, ch.kb * ch.eg], axis=1)) for t, ch in zip(t_invs, chains)]
        for ch, sol in zip(chains, sols):
            ch.u, ch.w = sol[:, :LANES], sol[:, LANES:]

        state = {}
        for wave in _waves(chunk_subs):
            cur = [ch for ch in chains if ch.j in wave]
            for ch in cur:
                if (ch.sub, ch.pair) not in state:
                    state[(ch.sub, ch.pair)] = s_scr[ch.sub, ch.pair]
            ws_qs = [_dot(jnp.concatenate([ch.w, ch.q * ch.eg], axis=0), state[(ch.sub, ch.pair)]) for ch in cur]
            v_news = [ch.u - x[:c2] for ch, x in zip(cur, ws_qs)]
            o_parts = [_dot(ch.qk, vn) for ch, vn in zip(cur, v_news)]
            s_parts = [_dot_tn(ch.k_dec, vn) for ch, vn in zip(cur, v_news)]
            for ch, x, op, sp in zip(cur, ws_qs, o_parts, s_parts):
                o_bd = x[c2:] + op
                ch.o = o_bd[:c] + o_bd[c:]
                state[(ch.sub, ch.pair)] = state[(ch.sub, ch.pair)] * ch.s_scale + sp
        for (sub, pair), val in state.items():
            s_scr[sub, pair] = val

        for ch in chains:
            rs = pl.ds(base + ch.j * c, c)
            lanes = slice(ch.pair * LANES, (ch.pair + 1) * LANES)
            z2 = z_ref[rs, lanes]
            o2 = ch.o * lax.rsqrt(_head_halves_mean(ch.o * ch.o) + NORM_EPS) * og2
            o_ref[rs, lanes] = o2 * (z2 * _sigmoid(z2))

    chunks_per_seq = tb // c
    if n_sub > 1 or chunks_per_seq <= group_chunks:
        run_group(0, [s for s in range(n_sub) for _ in range(chunks_per_seq)])
    else:
        g_rows = group_chunks * c

        def step(it, carry):
            run_group(pl.multiple_of(it * g_rows, g_rows), [0] * group_chunks)
            return carry

        lax.fori_loop(0, chunks_per_seq // group_chunks, step, 0)

    @pl.when(blk == last_blk)
    def _():
        for s in range(n_sub):
            for pair in range(n_heads // 2):
                s_bd = s_scr[s, pair]
                snew_ref[s, 2 * pair] = s_bd[:HEAD_DIM, :HEAD_DIM]
                snew_ref[s, 2 * pair + 1] = s_bd[HEAD_DIM:, HEAD_DIM:]


def _gdn(qkv, z, ab, conv_state, s0, conv_w, a_log, dt_bias, out_g, *, seq_rows, valid, chunk, tb, n_sub,
         group_chunks):
    n_seq, n_heads = s0.shape[0], s0.shape[1]
    w_all = n_heads * HEAD_DIM
    n_blk = seq_rows // tb
    assert n_sub == 1 or n_blk == 1
    assert n_seq % n_sub == 0 and (tb // chunk) % group_chunks == 0 or tb // chunk <= group_chunks
    _log2(tb), _log2(chunk)
    lane_h = np.arange(w_all) // HEAD_DIM
    bd = jnp.asarray(lane_h[:, None] == lane_h[None, :], BF16)
    ex = jnp.asarray((np.arange(LANES)[:, None] - n_heads) == lane_h[None, :], F32)
    row128 = lambda x: jnp.zeros((1, LANES), F32).at[0, :n_heads].set(x)
    rows_blk = n_sub * tb
    tok = lambda width: pl.BlockSpec((rows_blk, width), lambda s, b: (s * n_blk + b, 0))
    full = lambda a: pl.BlockSpec(a.shape, lambda s, b: (0,) * a.ndim)
    params = (conv_w, row128(a_log), row128(dt_bias), jnp.tile(out_g, 2).reshape(1, LANES), bd, ex)
    rows = n_seq * seq_rows
    return pl.pallas_call(
        functools.partial(_gdn_body, n_sub=n_sub, tb=tb, chunk=chunk, valid=valid, n_heads=n_heads,
                          group_chunks=group_chunks),
        grid=(n_seq // n_sub, n_blk),
        in_specs=[tok(3 * w_all), tok(w_all), tok(LANES),
                  pl.BlockSpec((n_sub, CONV_W - 1, 3 * w_all), lambda s, b: (s, 0, 0)),
                  pl.BlockSpec((n_sub, n_heads, HEAD_DIM, HEAD_DIM), lambda s, b: (s, 0, 0, 0))]
        + [full(a) for a in params],
        out_specs=[tok(w_all),
                   pl.BlockSpec((n_sub, CONV_W - 1, 3 * w_all), lambda s, b: (s, 0, 0)),
                   pl.BlockSpec((n_sub, n_heads, HEAD_DIM, HEAD_DIM), lambda s, b: (s, 0, 0, 0))],
        out_shape=[jax.ShapeDtypeStruct((rows, w_all), F32),
                   jax.ShapeDtypeStruct(conv_state.shape, F32),
                   jax.ShapeDtypeStruct(s0.shape, F32)],
        scratch_shapes=[pltpu.VMEM((n_sub, SUBLANES + tb, 3 * w_all), F32),
                        pltpu.VMEM((n_sub, n_heads // 2, LANES, LANES), F32),
                        pltpu.VMEM((rows_blk, 3 * w_all), F32),
                        pltpu.VMEM((rows_blk, w_all), F32), pltpu.VMEM((rows_blk, w_all), F32),
                        pltpu.VMEM((rows_blk, w_all), F32), pltpu.VMEM((rows_blk, w_all), F32),
                        pltpu.VMEM((rows_blk, LANES), F32)],
        compiler_params=_cparams("parallel", "arbitrary"),
        name="gdn_mixer",
    )(qkv, z, ab, conv_state, s0, *params)


def _rwkv_body(p_ref, sh_ref, s0_ref, mu_ref, w0_ref, a0_ref, kk_ref, ka_ref, rk_ref, lnw_ref, lnb_ref,
               w2_ref, a2_ref, g2_ref, bd_ref,
               o_ref, shnew_ref, snew_ref,
               xbuf, s_scr, r_s, k_s, v_s, kk_s, a_s, ld_s, gate_s, bonus_s,
               *, n_sub, tb, chunk, valid, n_heads, group_chunks):
    blk = pl.program_id(1)
    last_blk = pl.num_programs(1) - 1
    w_all = n_heads * HEAD_DIM
    c = chunk

    @pl.when(blk == 0)
    def _():
        for s in range(n_sub):
            xbuf[s, SUBLANES - 1:SUBLANES, :] = sh_ref[s]
            for pair in range(n_heads // 2):
                s_scr[s, pair] = _pair_state(s0_ref[s, 2 * pair], s0_ref[s, 2 * pair + 1])

    shifted = []
    for s in range(n_sub):
        xbuf[s, SUBLANES:SUBLANES + tb, :] = p_ref[s * tb:(s + 1) * tb, :]
        shifted.append(xbuf[s, SUBLANES - 1:SUBLANES - 1 + tb, :])
        xbuf[s, SUBLANES - 1:SUBLANES, :] = xbuf[s, SUBLANES - 1 + valid:SUBLANES + valid, :]
    shifted = shifted[0] if n_sub == 1 else jnp.concatenate(shifted, axis=0)

    @pl.when(blk == last_blk)
    def _():
        for s in range(n_sub):
            shnew_ref[s] = xbuf[s, SUBLANES - 1:SUBLANES, :]

    p = p_ref[...]
    xs = p + (shifted - p) * mu_ref[...]
    r = xs[:, 0:w_all]
    k = xs[:, w_all:2 * w_all]
    v = xs[:, 2 * w_all:3 * w_all]
    lo = xs[:, 3 * w_all:3 * w_all + LANES]
    g_lo = xs[:, 3 * w_all + LANES:]
    w_log = -_softplus(-(w0_ref[...] + _dot(jnp.tanh(lo), w2_ref[...]))) - 0.5
    log_decay = -jnp.exp(w_log)
    a = _sigmoid(a0_ref[...] + _dot(lo, a2_ref[...]))
    gate_s[...] = _dot(_sigmoid(g_lo), g2_ref[...])
    bd = bd_ref[...]
    kkx = k * kk_ref[...]
    kk = kkx * lax.rsqrt(_dot_split(kkx * kkx, bd) + L2_EPS)
    k = k * (1.0 + (a - 1.0) * ka_ref[...])
    bonus_s[...] = _dot_split(r * k * rk_ref[...], bd) * v
    if valid < tb:
        live = (_iota2(r.shape, 0) & (tb - 1)) < valid
        log_decay = jnp.where(live, log_decay, 0.0)
        kk = jnp.where(live, kk, 0.0)
        v = jnp.where(live, v, 0.0)
    r_s[...] = r
    k_s[...] = k
    v_s[...] = v
    kk_s[...] = kk
    a_s[...] = a
    ld_s[...] = log_decay

    c2 = 2 * c
    incl = _tril_block_diag(c2, c) > 0.5
    strict = jnp.logical_and(incl, _iota2((c2, c2), 0) != _iota2((c2, c2), 1))

    def run_group(base, chunk_subs):
        g_rows = len(chunk_subs) * c
        rows_g = pl.ds(base, g_rows)
        ld = ld_s[rows_g, :]
        gcum = _dot_exact(_tril_block_diag(g_rows, c), ld)
        e_pos = jnp.exp(gcum)
        e_neg = jnp.exp(-gcum)
        kk_g = kk_s[rows_g, :]
        r_t = r_s[rows_g, :] * e_pos
        b_t = kk_g * jnp.exp(gcum - ld)
        a_t = -(kk_g * a_s[rows_g, :]) * e_neg
        k_t = k_s[rows_g, :] * e_neg
        v_g = v_s[rows_g, :]
        chains = []
        for j, sub in enumerate(chunk_subs):
            js = slice(j * c, (j + 1) * c)
            for pair in range(n_heads // 2):
                lanes = slice(pair * LANES, (pair + 1) * LANES)
                ch = types.SimpleNamespace(j=j, sub=sub, pair=pair)
                ch.b, ch.r, ch.v = _pair_rows(b_t[js, lanes]), _pair_rows(r_t[js, lanes]), _pair_rows(v_g[js, lanes])
                ch.left = jnp.concatenate([ch.b, ch.r], axis=0)
                ch.right = jnp.concatenate([_pair_rows(a_t[js, lanes]), _pair_rows(k_t[js, lanes])], axis=0)
                ch.e_last = e_pos[(j + 1) * c - 1:(j + 1) * c, lanes]
                chains.append(ch)
        ms = [_dot_nt(ch.left, ch.right) for ch in chains]
        for ch, m in zip(chains, ms):
            ch.a_ab = jnp.where(strict, m[:c2, :c2], 0.0)
            ch.a_kb = jnp.where(strict, m[:c2, c2:], 0.0)
            ch.q_ak = jnp.concatenate([jnp.where(incl, m[c2:, :c2], 0.0), jnp.where(incl, m[c2:, c2:], 0.0)], axis=1)
        akvs = [_dot(ch.a_kb, ch.v) for ch in chains]
        t_invs = _inv_i_minus_many([ch.a_ab for ch in chains], c)
        tbs = [_dot(t, jnp.concatenate([ch.b, akv], axis=1)) for t, ch, akv in zip(t_invs, chains, akvs)]
        for ch, x in zip(chains, tbs):
            ch.tb_r = jnp.concatenate([x[:, :LANES], ch.r], axis=0)
            ch.takv = x[:, LANES:]

        state = {}
        for wave in _waves(chunk_subs):
            cur = [ch for ch in chains if ch.j in wave]
            for ch in cur:
                if (ch.sub, ch.pair) not in state:
                    state[(ch.sub, ch.pair)] = s_scr[ch.sub, ch.pair]
            lss = [_dot_nt(ch.tb_r, state[(ch.sub, ch.pair)]) for ch in cur]
            uvs = [jnp.concatenate([x[:c2] + ch.takv, ch.v], axis=0) for ch, x in zip(cur, lss)]
            o_parts = [_dot(ch.q_ak, uv) for ch, uv in zip(cur, uvs)]
            s_parts = [_dot_tn(uv, ch.right * ch.e_last) for ch, uv in zip(cur, uvs)]
            for ch, x, op, sp in zip(cur, lss, o_parts, s_parts):
                o_bd = x[c2:] + op
                o = o_bd[:c] + o_bd[c:]
                d = o - _head_halves_mean(o)
                ch.o = d * lax.rsqrt(_head_halves_mean(d * d) + RWKV_GN_EPS)
                state[(ch.sub, ch.pair)] = state[(ch.sub, ch.pair)] * ch.e_last + sp
        for (sub, pair), val in state.items():
            s_scr[sub, pair] = val

        for ch in chains:
            rs = pl.ds(base + ch.j * c, c)
            lanes = slice(ch.pair * LANES, (ch.pair + 1) * LANES)
            on = ch.o * lnw_ref[:, lanes] + lnb_ref[:, lanes]
            o_ref[rs, lanes] = (on + bonus_s[rs, lanes]) * gate_s[rs, lanes]

    chunks_per_seq = tb // c
    if n_sub > 1 or chunks_per_seq <= group_chunks:
        run_group(0, [s for s in range(n_sub) for _ in range(chunks_per_seq)])
    else:
        g_rows = group_chunks * c

        def step(it, carry):
            run_group(pl.multiple_of(it * g_rows, g_rows), [0] * group_chunks)
            return carry

        lax.fori_loop(0, chunks_per_seq // group_chunks, step, 0)

    @pl.when(blk == last_blk)
    def _():
        for s in range(n_sub):
            for pair in range(n_heads // 2):
                s_bd = s_scr[s, pair]
                snew_ref[s, 2 * pair] = s_bd[:HEAD_DIM, :HEAD_DIM]
                snew_ref[s, 2 * pair + 1] = s_bd[HEAD_DIM:, HEAD_DIM:]


def _rwkv(p, shift_state, s0, mu, w0, w2, a0, a2, g2, k_k, k_a, r_k, ln_w, ln_b, *, seq_rows, valid, chunk, tb,
          n_sub, group_chunks):
    n_seq, n_heads = s0.shape[0], s0.shape[1]
    w_all = n_heads * HEAD_DIM
    width = p.shape[1]
    n_blk = seq_rows // tb
    assert n_sub == 1 or n_blk == 1
    _log2(tb), _log2(chunk)
    lane_h = np.arange(w_all) // HEAD_DIM
    bd = jnp.asarray(lane_h[:, None] == lane_h[None, :], BF16)
    lora = w2.shape[0]
    w2p = jnp.zeros((LANES, w_all), BF16).at[:lora].set(w2.astype(BF16))
    a2p = jnp.zeros((LANES, w_all), BF16).at[lora:lora + a2.shape[0]].set(a2.astype(BF16))
    r1 = lambda x: x.reshape(1, -1)
    params = (r1(mu), r1(w0), r1(a0), r1(k_k), r1(k_a), r1(r_k), r1(ln_w), r1(ln_b),
              w2p, a2p, g2.astype(BF16), bd)
    rows_blk = n_sub * tb
    tok = lambda wd: pl.BlockSpec((rows_blk, wd), lambda s, b: (s * n_blk + b, 0))
    full = lambda arr: pl.BlockSpec(arr.shape, lambda s, b: (0,) * arr.ndim)
    rows = n_seq * seq_rows
    return pl.pallas_call(
        functools.partial(_rwkv_body, n_sub=n_sub, tb=tb, chunk=chunk, valid=valid, n_heads=n_heads,
                          group_chunks=group_chunks),
        grid=(n_seq // n_sub, n_blk),
        in_specs=[tok(width),
                  pl.BlockSpec((n_sub, 1, width), lambda s, b: (s, 0, 0)),
                  pl.BlockSpec((n_sub, n_heads, HEAD_DIM, HEAD_DIM), lambda s, b: (s, 0, 0, 0))]
        + [full(arr) for arr in params],
        out_specs=[tok(w_all),
                   pl.BlockSpec((n_sub, 1, width), lambda s, b: (s, 0, 0)),
                   pl.BlockSpec((n_sub, n_heads, HEAD_DIM, HEAD_DIM), lambda s, b: (s, 0, 0, 0))],
        out_shape=[jax.ShapeDtypeStruct((rows, w_all), F32),
                   jax.ShapeDtypeStruct(shift_state.shape, F32),
                   jax.ShapeDtypeStruct(s0.shape, F32)],
        scratch_shapes=[pltpu.VMEM((n_sub, SUBLANES + tb, width), F32),
                        pltpu.VMEM((n_sub, n_heads // 2, LANES, LANES), F32)]
        + [pltpu.VMEM((rows_blk, w_all), F32) for _ in range(8)],
        compiler_params=_cparams("parallel", "arbitrary"),
        name="rwkv7_mixer",
    )(p, shift_state, s0, *params)


def _sb_prompt_body(q_ref, k_ref, v_ref, u_ref, o_ref, *, tq, tk, n_pairs):
    qi = pl.program_id(2)
    n_diag = tq // tk
    n_h = 2 * n_pairs
    lane = _iota2((tq, LANES), 1)
    q_heads = []
    for pair in range(n_pairs):
        q = q_ref[:, pair * LANES:(pair + 1) * LANES] * ATTN_SCALE
        q_heads += [jnp.where(lane < HEAD_DIM, q, 0.0).astype(BF16),
                    jnp.where(lane >= HEAD_DIM, q, 0.0).astype(BF16)]
    qrow = _iota2((tq, tk), 0)
    kidx = _iota2((tq, tk), 1)
    u = u_ref[...]

    def visit(k0, carry, mask):
        ks = [k_ref[pl.ds(k0, tk), pair * LANES:(pair + 1) * LANES].astype(BF16) for pair in range(n_pairs)]
        vs = [v_ref[pl.ds(k0, tk), pair * LANES:(pair + 1) * LANES].astype(BF16) for pair in range(n_pairs)]
        zs = [lax.dot_general(q_heads[hh], ks[hh // 2], (((1,), (1,)), ((), ())), preferred_element_type=F32)
              for hh in range(n_h)]
        sps = [_softplus(z) for z in zs]
        lss = [-sp if mask is None else jnp.where(mask, -sp, 0.0) for sp in sps]
        cums = [_dot_split(ls, u) for ls in lss]
        ws = []
        for hh in range(n_h):
            w = jnp.exp(zs[hh] - sps[hh] + (carry[2 * hh] + cums[hh]))
            ws.append((w if mask is None else jnp.where(mask, w, 0.0)).astype(BF16))
        pvs = [jnp.dot(ws[hh], vs[hh // 2], preferred_element_type=F32) for hh in range(n_h)]
        new = []
        for hh in range(n_h):
            new += [carry[2 * hh] + jnp.sum(lss[hh], axis=-1, keepdims=True), carry[2 * hh + 1] + pvs[hh]]
        return tuple(new)

    carry = (jnp.zeros((tq, 1), F32), jnp.zeros((tq, LANES), F32)) * n_h
    for d in range(n_diag):
        off = (n_diag - 1 - d) * tk
        k0 = pl.multiple_of(qi * tq + off, tk)
        carry = visit(k0, carry, (kidx + off) < qrow)

    def alive(state):
        top = jnp.max(state[1])
        for hh in range(1, n_h):
            top = jnp.maximum(top, jnp.max(state[1 + 2 * hh]))
        return jnp.logical_and(state[0] < qi * n_diag, top > SB_DEAD_LOG)

    def body(state):
        jj = state[0]
        k0 = pl.multiple_of((qi * n_diag - 1 - jj) * tk, tk)
        return (jj + 1,) + visit(k0, state[1:], None)

    res = lax.while_loop(alive, body, (jnp.int32(0),) + carry)
    for pair in range(n_pairs):
        o_ref[:, pair * LANES:(pair + 1) * LANES] = jnp.where(lane < HEAD_DIM, res[2 + 4 * pair], res[4 + 4 * pair])


def _sb_prompt(p, n_batch, seq, col_q, col_k, col_v, n_heads):
    tq = _pick_tile(seq, (256, 128, 64, 32, 16, 8))
    tk = _pick_tile(tq, (128, 64, 32, 16, 8))
    n_q = seq // tq
    n_pairs = 2 if n_heads % 4 == 0 else 1
    wblk = n_pairs * LANES
    n_grp = n_heads * HEAD_DIM // wblk
    u = jnp.asarray(np.arange(tk)[:, None] > np.arange(tk)[None, :], BF16)
    return pl.pallas_call(
        functools.partial(_sb_prompt_body, tq=tq, tk=tk, n_pairs=n_pairs),
        grid=(n_batch, n_grp, n_q),
        in_specs=[
            pl.BlockSpec((tq, wblk), lambda b, g, i: (b * n_q + i, col_q // wblk + g)),
            pl.BlockSpec((seq, wblk), lambda b, g, i: (b, col_k // wblk + g)),
            pl.BlockSpec((seq, wblk), lambda b, g, i: (b, col_v // wblk + g)),
            pl.BlockSpec((tk, tk), lambda b, g, i: (0, 0)),
        ],
        out_specs=pl.BlockSpec((tq, wblk), lambda b, g, i: (b * n_q + i, g)),
        out_shape=jax.ShapeDtypeStruct((n_batch * seq, n_heads * HEAD_DIM), F32),
        compiler_params=_cparams("parallel", "parallel", "arbitrary"),
        name="sb_attn_prompt",
    )(p, p, p, u)


def _lambda_of(lam_ref, lam_init):
    lv = lam_ref[...]
    return (jnp.exp(jnp.sum(lv[0:1] * lv[1:2], axis=-1, keepdims=True))
            - jnp.exp(jnp.sum(lv[2:3] * lv[3:4], axis=-1, keepdims=True)) + lam_init)


def _diff_prompt_body(q_ref, k_ref, v_ref, lam_ref, sn_ref, o_ref, *, tq, tk, lam_init):
    qi = pl.program_id(2)
    lane = _iota2((tq, LANES), 1)
    q = q_ref[...] * ATTN_SCALE
    q_maps = [jnp.where(lane < HEAD_DIM, q, 0.0).astype(BF16),
              jnp.where(lane >= HEAD_DIM, q, 0.0).astype(BF16)]

    def visit(k0, carry, mask):
        k = k_ref[pl.ds(k0, tk), :].astype(BF16)
        v = v_ref[pl.ds(k0, tk), :].astype(BF16)
        ss = [lax.dot_general(qm, k, (((1,), (1,)), ((), ())), preferred_element_type=F32) for qm in q_maps]
        if mask is not None:
            ss = [jnp.where(mask, s, NEG_BIG) for s in ss]
        m_news = [jnp.maximum(carry[3 * mm], jnp.max(ss[mm], axis=-1, keepdims=True)) for mm in range(2)]
        prs = [jnp.exp(ss[mm] - m_news[mm]) for mm in range(2)]
        if mask is not None:
            prs = [jnp.where(mask, pr, 0.0) for pr in prs]
        pvs = [jnp.dot(pr.astype(BF16), v, preferred_element_type=F32) for pr in prs]
        new = []
        for mm in range(2):
            m_run, l_run, acc = carry[3 * mm:3 * mm + 3]
            corr = jnp.exp(m_run - m_news[mm])
            new += [m_news[mm], l_run * corr + jnp.sum(prs[mm], axis=-1, keepdims=True), acc * corr + pvs[mm]]
        return tuple(new)

    one = (jnp.full((tq, 1), NEG_BIG, F32), jnp.zeros((tq, 1), F32), jnp.zeros((tq, LANES), F32))
    n_full = (qi * tq) // tk
    res = lax.fori_loop(0, n_full, lambda kb, cr: visit(pl.multiple_of(kb * tk, tk), cr, None), one * 2)
    k0 = pl.multiple_of(n_full * tk, tk)
    mask = (k0 + _iota2((tq, tk), 1)) <= (qi * tq + _iota2((tq, tk), 0))
    res = visit(k0, res, mask)
    lam = _lambda_of(lam_ref, lam_init)
    o = res[2] / res[1] - lam * (res[5] / res[4])
    o_ref[...] = _rms(o) * sn_ref[...] * (1.0 - lam_init)


def _diff_prompt(p, n_batch, seq, col_q, col_k, col_v, n_heads, lam_vecs, sub_norm, lam_init):
    tq = _pick_tile(seq, (512, 256, 128, 64, 32, 16, 8))
    tk = _pick_tile(seq, (512, 256, 128, 64, 32, 16, 8))
    assert tk % tq == 0
    n_q = seq // tq
    col_q, col_k, col_v = col_q // LANES, col_k // LANES, col_v // LANES
    return pl.pallas_call(
        functools.partial(_diff_prompt_body, tq=tq, tk=tk, lam_init=lam_init),
        grid=(n_batch, n_heads, n_q),
        in_specs=[
            pl.BlockSpec((tq, LANES), lambda b, h, i: (b * n_q + i, col_q + h)),
            pl.BlockSpec((seq, LANES), lambda b, h, i: (b, col_k + h)),
            pl.BlockSpec((seq, LANES), lambda b, h, i: (b, col_v + h)),
            pl.BlockSpec(lam_vecs.shape, lambda b, h, i: (0, 0)),
            pl.BlockSpec((1, LANES), lambda b, h, i: (0, 0)),
        ],
        out_specs=pl.BlockSpec((tq, LANES), lambda b, h, i: (b * n_q + i, h)),
        out_shape=jax.ShapeDtypeStruct((n_batch * seq, n_heads * LANES), F32),
        compiler_params=_cparams("parallel", "parallel", "arbitrary"),
        name="diff_attn_prompt",
    )(p, p, p, lam_vecs, sub_norm.reshape(1, LANES))


def _block_diag_queries(q, n_groups):
    lane_g = _iota2(q.shape, 1) >> HEAD_SHIFT
    return jnp.concatenate([jnp.where(lane_g == g, q, 0.0) for g in range(n_groups)], axis=0)


def _sb_sample_body(pt_ref, q_ref, kn_ref, vn_ref, *rest, n_new, n_groups, n_pages):
    del pt_ref
    kt_refs, vt_refs = rest[:n_pages], rest[n_pages:2 * n_pages]
    u_ref, o_ref = rest[2 * n_pages:]
    rows = n_groups * SUBLANES
    width = n_groups * HEAD_DIM
    qrow = _iota2((rows, 1), 0) & (SUBLANES - 1)

    qbd = _block_diag_queries(q_ref[...] * ATTN_SCALE, n_groups)
    c = jnp.zeros((rows, 1), F32)
    acc = jnp.zeros((rows, width), F32)
    kn = kn_ref[...]
    vn = vn_ref[...]
    for t in range(n_new - 1, -1, -1):
        z = jnp.sum(qbd * kn[t:t + 1, :], axis=-1, keepdims=True)
        vis = t < qrow
        sp = _softplus(z)
        acc = acc + jnp.where(vis, jnp.exp(z - sp + c), 0.0) * vn[t:t + 1, :]
        c = c + jnp.where(vis, -sp, 0.0)

    page = lambda x, j: x[:, j * LANES:(j + 1) * LANES]
    kt_all = jnp.concatenate([r[...].reshape(width, LANES).astype(BF16) for r in kt_refs], axis=1)
    vt_all = jnp.concatenate([r[...].reshape(width, LANES).astype(BF16) for r in vt_refs], axis=1)
    z = jnp.dot(qbd.astype(BF16), kt_all, preferred_element_type=F32)
    sp = _softplus(z)
    sp_rows = jnp.concatenate([page(sp, j) for j in range(n_pages)], axis=0)
    cum_rows = _dot_split(sp_rows, u_ref[...])
    tot_rows = jnp.sum(sp_rows, axis=-1, keepdims=True)
    ws = [None] * n_pages
    for j in range(n_pages - 1, -1, -1):
        rj = slice(j * rows, (j + 1) * rows)
        ws[j] = jnp.exp(page(z, j) - page(sp, j) + (c - cum_rows[rj, :]))
        c = c - tot_rows[rj, :]
    acc = acc + lax.dot_general(jnp.concatenate(ws, axis=1).astype(BF16), vt_all, (((1,), (1,)), ((), ())),
                                preferred_element_type=F32)

    lane_g = _iota2((SUBLANES, width), 1) >> HEAD_SHIFT
    out = jnp.zeros((SUBLANES, width), F32)
    for g in range(n_groups):
        out = out + jnp.where(lane_g == g, acc[g * SUBLANES:(g + 1) * SUBLANES, :], 0.0)
    o_ref[...] = out


def _sb_sample(page_table, p3, col0, cache_kt, cache_vt, layer, n_new):
    n_b, n_pages = page_table.shape
    n_heads = cache_kt.shape[2]
    width = n_heads * HEAD_DIM
    page = cache_kt.shape[-1]
    u = jnp.asarray(np.arange(page)[:, None] > np.arange(page)[None, :], BF16)
    tok = lambda c: pl.BlockSpec((None, SUBLANES, width), lambda b, pt: (b, 0, c))
    pg = [pl.BlockSpec((None, None, n_heads, HEAD_DIM, page), lambda b, pt, j=j: (layer, pt[b, j], 0, 0, 0))
          for j in range(n_pages)]
    return pl.pallas_call(
        functools.partial(_sb_sample_body, n_new=n_new, n_groups=n_heads, n_pages=n_pages),
        grid_spec=pltpu.PrefetchScalarGridSpec(
            num_scalar_prefetch=1,
            grid=(n_b,),
            in_specs=[tok(col0), tok(col0 + 1), tok(col0 + 2)] + pg + pg
            + [pl.BlockSpec((page, page), lambda b, pt: (0, 0))],
            out_specs=tok(0)),
        out_shape=jax.ShapeDtypeStruct((n_b, SUBLANES, width), F32),
        compiler_params=_cparams("parallel"),
        name="sb_attn_sample",
    )(page_table, p3, p3, p3, *([cache_kt] * n_pages), *([cache_vt] * n_pages), u)


def _diff_sample_body(pt_ref, q_ref, kn_ref, vn_ref, *rest, n_new, n_heads, n_pages, lam_init):
    del pt_ref
    kt_refs, v_refs = rest[:n_pages], rest[n_pages:2 * n_pages]
    ex_ref, lam_ref, sn_ref, o_ref = rest[2 * n_pages:]
    n_groups = 2 * n_heads
    rows = n_groups * SUBLANES
    hrows = 2 * SUBLANES
    qrow = _iota2((rows, 1), 0) & (SUBLANES - 1)

    qbd = _block_diag_queries(q_ref[...] * ATTN_SCALE, n_groups)
    kn = kn_ref[...]
    vn = vn_ref[...]
    m_run = jnp.full((rows, 1), NEG_BIG, F32)
    l_run = jnp.zeros((rows, 1), F32)
    acc = jnp.zeros((rows, LANES), F32)
    for t in range(n_new):
        s = jnp.sum(qbd * kn[t:t + 1, :], axis=-1, keepdims=True)
        vis = t <= qrow
        m_new = jnp.maximum(m_run, jnp.where(vis, s, NEG_BIG))
        corr = jnp.exp(m_run - m_new)
        pr = jnp.where(vis, jnp.exp(s - m_new), 0.0)
        v_rows = jnp.concatenate(
            [jnp.broadcast_to(vn[t:t + 1, h * LANES:(h + 1) * LANES], (hrows, LANES)) for h in range(n_heads)],
            axis=0)
        l_run = l_run * corr + pr
        acc = acc * corr + pr * v_rows
        m_run = m_new

    kt_all = jnp.concatenate([r[...].reshape(n_groups * HEAD_DIM, LANES).astype(BF16) for r in kt_refs], axis=1)
    s = jnp.dot(qbd.astype(BF16), kt_all, preferred_element_type=F32)
    m_new = jnp.maximum(m_run, jnp.max(s, axis=-1, keepdims=True))
    corr = jnp.exp(m_run - m_new)
    pr = jnp.exp(s - m_new)
    l_run = l_run * corr + jnp.sum(pr, axis=-1, keepdims=True)
    pr_rows = jnp.concatenate([pr[:, j * LANES:(j + 1) * LANES] for j in range(n_pages)], axis=0)
    spread = jnp.dot(pr_rows.astype(BF16), ex_ref[...], preferred_element_type=F32)
    own = ((_iota2((rows, LANES * n_heads), 1) & (n_heads - 1))
           == (_iota2((rows, LANES * n_heads), 0) >> _log2(hrows)))
    pr_exp = jnp.concatenate(
        [jnp.where(own, spread[j * rows:(j + 1) * rows, :], 0.0).astype(BF16) for j in range(n_pages)], axis=1)
    v_all = jnp.concatenate([r[...].astype(BF16) for r in v_refs], axis=0)
    o_all = (acc * corr + jnp.dot(pr_exp, v_all, preferred_element_type=F32)) / l_run
    lam = _lambda_of(lam_ref, lam_init)
    outs = []
    for h in range(n_heads):
        o = o_all[h * hrows:h * hrows + SUBLANES, :] - lam * o_all[h * hrows + SUBLANES:(h + 1) * hrows, :]
        outs.append(_rms(o) * sn_ref[...] * (1.0 - lam_init))
    o_ref[...] = jnp.concatenate(outs, axis=1)


def _diff_sample(page_table, p3, col0, cache_kt, cache_v, layer, n_new, lam_vecs, sub_norm, lam_init):
    n_b, n_pages = page_table.shape
    n_heads = cache_kt.shape[2]
    page = cache_kt.shape[-1]
    width = n_heads * 2 * HEAD_DIM
    tok = lambda c: pl.BlockSpec((None, SUBLANES, width), lambda b, pt: (b, 0, c))
    kpg = [pl.BlockSpec((None, None, n_heads, 2, HEAD_DIM, page), lambda b, pt, j=j: (layer, pt[b, j], 0, 0, 0, 0))
           for j in range(n_pages)]
    vpg = [pl.BlockSpec((None, None, page * n_heads, LANES), lambda b, pt, j=j: (layer, pt[b, j], 0, 0))
           for j in range(n_pages)]
    _log2(n_heads)
    ex = jnp.asarray(np.arange(page)[:, None] == (np.arange(page * n_heads)[None, :] // n_heads), BF16)
    cache_v2 = cache_v.reshape(cache_v.shape[:2] + (page * n_heads, LANES))
    return pl.pallas_call(
        functools.partial(_diff_sample_body, n_new=n_new, n_heads=n_heads, n_pages=n_pages, lam_init=lam_init),
        grid_spec=pltpu.PrefetchScalarGridSpec(
            num_scalar_prefetch=1,
            grid=(n_b,),
            in_specs=[tok(col0), tok(col0 + 1), tok(col0 + 2)] + kpg + vpg
            + [pl.BlockSpec(ex.shape, lambda b, pt: (0, 0)),
               pl.BlockSpec(lam_vecs.shape, lambda b, pt: (0, 0)), pl.BlockSpec((1, LANES), lambda b, pt: (0, 0))],
            out_specs=tok(0)),
        out_shape=jax.ShapeDtypeStruct((n_b, SUBLANES, width), F32),
        compiler_params=_cparams("parallel"),
        name="diff_attn_sample",
    )(page_table, p3, p3, p3, *([cache_kt] * n_pages), *([cache_v2] * n_pages),
      ex, lam_vecs, sub_norm.reshape(1, LANES))


def _even_mix(hp, hs, norm_g, dims, states, prm):
    bp, lp, bs, ls, ls_pad = dims
    (w_in, w_out, conv_w, a_log, dt_bias, out_g, mu, w0, w2, a0, a2, g2, k_k, k_a, r_k, ln_w, ln_b) = prm
    conv_s, gdn_s, shift_s, rwkv_s = states
    n_heads = a_log.shape[0]
    w_gdn = n_heads * HEAD_DIM
    gdn_in = 4 * w_gdn + 2 * n_heads
    ab_w = jnp.zeros((w_in.shape[0], LANES), F32).at[:, :2 * n_heads].set(w_in[:, 4 * w_gdn:gdn_in])
    w_cat = jnp.concatenate([w_in[:, :4 * w_gdn], w_in[:, gdn_in:], ab_w], axis=1).astype(BF16)
    rwkv_in = w_in.shape[1] - gdn_in
    splits = (3 * w_gdn, w_gdn, rwkv_in, LANES)
    w_out_b = w_out.astype(BF16)
    gdn_prm = (conv_w, a_log, dt_bias, out_g)
    rwkv_prm = (mu, w0, w2, a0, a2, g2, k_k, k_a, r_k, ln_w, ln_b)

    chunk_p = min(64, lp)
    tb_p = _pick_tile(lp, (256, 128, 64)) if lp >= 64 else lp
    cfg_p = dict(seq_rows=lp, valid=tb_p, chunk=chunk_p, tb=tb_p, n_sub=1, group_chunks=2)
    zeros = lambda shape: jnp.zeros(shape, F32)
    qkv, z, p_rwkv, ab = _proj_even(hp, norm_g, w_cat, splits)
    o_a, conv_p, gdn_p = _gdn(qkv, z, ab, zeros((bp,) + conv_s.shape[1:]), zeros((bp,) + gdn_s.shape[1:]),
                              *gdn_prm, **cfg_p)
    o_b, shift_p, rwkv_p = _rwkv(p_rwkv, zeros((bp,) + shift_s.shape[1:]), zeros((bp,) + rwkv_s.shape[1:]),
                                 *rwkv_prm, **cfg_p)
    hp = _outproj(hp, o_a, o_b, w_out_b[:w_gdn], w_out_b[w_gdn:])

    cfg_s = dict(seq_rows=ls_pad, valid=ls, chunk=ls_pad, tb=ls_pad, n_sub=_pick_tile(bs, (4, 2, 1)),
                 group_chunks=1)
    qkv, z, p_rwkv, ab = _proj_even(hs, norm_g, w_cat, splits)
    o_a, conv_n, gdn_n = _gdn(qkv, z, ab, conv_s, gdn_s, *gdn_prm, **cfg_s)
    o_b, shift_n, rwkv_n = _rwkv(p_rwkv, shift_s, rwkv_s, *rwkv_prm, **cfg_s)
    hs = _outproj(hs, o_a, o_b, w_out_b[:w_gdn], w_out_b[w_gdn:])
    return hp, hs, (conv_p, gdn_p, shift_p, rwkv_p), (conv_n, gdn_n, shift_n, rwkv_n)


def _odd_mix(hp, hs, norm_g, dims, caches, page_table, layer_idx, pos_p, pos_s, freq, prm, lam_init):
    bp, lp, bs, ls, ls_pad = dims
    assert ls_pad == SUBLANES
    w_in, w_out, lam_vecs, sub_norm = prm
    ck_sb, cv_sb, ck_d, cv_d = caches
    h_sb = ck_sb.shape[3]
    h_d = ck_d.shape[3]
    w_sb = h_sb * HEAD_DIM
    w_d = h_d * 2 * HEAD_DIM
    assert w_sb == w_d
    d0 = 3 * w_sb
    w_in_b = w_in.astype(BF16)
    w_out_b = w_out.astype(BF16)

    pp = _proj_odd(hp, norm_g, w_in_b, pos_p, freq, d0, d0 + 2 * w_d)
    o_sb = _sb_prompt(pp, bp, lp, 0, w_sb, 2 * w_sb, h_sb)
    o_d = _diff_prompt(pp, bp, lp, d0, d0 + w_d, d0 + 2 * w_d, h_d, lam_vecs, sub_norm, lam_init)
    hp = _outproj(hp, o_sb, o_d, w_out_b[:w_sb], w_out_b[w_sb:])

    ps = _proj_odd(hs, norm_g, w_in_b, pos_s, freq, d0, d0 + 2 * w_d)
    ps3 = ps.reshape(bs, ls_pad, ps.shape[-1])
    o_sb = _sb_sample(page_table, ps3, 0, jnp.transpose(ck_sb, (0, 1, 3, 4, 2)),
                      jnp.transpose(cv_sb, (0, 1, 3, 4, 2)), layer_idx, ls)
    o_d = _diff_sample(page_table, ps3, d0 // w_d, jnp.transpose(ck_d, (0, 1, 3, 4, 5, 2)), cv_d, layer_idx, ls,
                       lam_vecs, sub_norm, lam_init)
    hs = _outproj(hs, o_sb.reshape(bs * ls_pad, w_sb), o_d.reshape(bs * ls_pad, w_d),
                  w_out_b[:w_sb], w_out_b[w_sb:])

    def split(rows):
        b, l = rows.shape[:2]
        k_sb = rows[..., w_sb:2 * w_sb].reshape(b, l, h_sb, HEAD_DIM)
        v_sb = rows[..., 2 * w_sb:3 * w_sb].reshape(b, l, h_sb, HEAD_DIM)
        k_d = rows[..., d0 + w_d:d0 + 2 * w_d].reshape(b, l, h_d, 2, HEAD_DIM)
        v_d = rows[..., d0 + 2 * w_d:d0 + 3 * w_d].reshape(b, l, h_d, 2 * HEAD_DIM)
        return k_sb, v_sb, k_d, v_d

    return hp, hs, split(pp.reshape(bp, lp, pp.shape[-1])), split(ps3[:, :ls])


def kernel(x_prompt, x_sample, cache_k_sb, cache_v_sb, cache_k_diff, cache_v_diff, page_table, state_gdn, state_gdn_conv, state_rwkv, state_rwkv_shift, ffn_norm, ffn_w_gate, ffn_w_up, ffn_w_down, mix_norm, final_norm, ev_w_in, ev_w_out, gdn_conv_w, gdn_a_log, gdn_dt_bias, gdn_out_norm, rwkv_mu, rwkv_w0, rwkv_w2, rwkv_a0, rwkv_a2, rwkv_g2, rwkv_k_k, rwkv_k_a, rwkv_r_k, rwkv_ln_w, rwkv_ln_b, od_w_in, od_w_out, diff_lambda, diff_norm):
    bp, lp, d = x_prompt.shape
    bs, ls, _ = x_sample.shape
    depth = ffn_norm.shape[0]
    ls_pad = -(-ls // SUBLANES) * SUBLANES
    dims = (bp, lp, bs, ls, ls_pad)
    past_len = page_table.shape[1] * cache_k_sb.shape[2]

    hp = x_prompt.reshape(bp * lp, d)
    hs = jnp.pad(x_sample, ((0, 0), (0, ls_pad - ls), (0, 0))).reshape(bs * ls_pad, d)
    pos_p = jnp.tile(jnp.arange(lp), bp).astype(F32).reshape(-1, 1)
    pos_s = jnp.tile(past_len + jnp.arange(ls_pad), bs).astype(F32).reshape(-1, 1)
    half = ROPE_DIM // 2
    inv_freq = ROPE_THETA ** (-jnp.arange(half, dtype=F32) * 2.0 / ROPE_DIM)
    lane = np.arange(LANES) % HEAD_DIM
    freq = jnp.where(lane < ROPE_DIM, inv_freq[lane % half], 0.0).astype(F32).reshape(1, LANES)

    wg, wu, wd = (w.astype(BF16) for w in (ffn_w_gate, ffn_w_up, ffn_w_down))
    names = ("k_sb", "v_sb", "k_diff", "v_diff", "gdn_conv", "gdn", "rwkv_shift", "rwkv")
    new_p = {n: [] for n in names}
    new_s = {n: [] for n in names}
    for l in range(depth):
        i = l // 2
        hp = _ffn_half_step(hp, ffn_norm[l, 0], wg[l, 0], wu[l, 0], wd[l, 0])
        hs = _ffn_half_step(hs, ffn_norm[l, 0], wg[l, 0], wu[l, 0], wd[l, 0])
        if l % 2 == 0:
            prm = (ev_w_in[i], ev_w_out[i], gdn_conv_w[i], gdn_a_log[i], gdn_dt_bias[i], gdn_out_norm[i],
                   rwkv_mu[i], rwkv_w0[i], rwkv_w2[i], rwkv_a0[i], rwkv_a2[i], rwkv_g2[i],
                   rwkv_k_k[i], rwkv_k_a[i], rwkv_r_k[i], rwkv_ln_w[i], rwkv_ln_b[i])
            states = (state_gdn_conv[i], state_gdn[i], state_rwkv_shift[i], state_rwkv[i])
            hp, hs, st_p, st_s = _even_mix(hp, hs, mix_norm[l], dims, states, prm)
            for n, a_p, a_s in zip(("gdn_conv", "gdn", "rwkv_shift", "rwkv"), st_p, st_s):
                new_p[n].append(a_p)
                new_s[n].append(a_s)
        else:
            lam_init = 0.8 - 0.6 * math.exp(-0.3 * l)
            prm = (od_w_in[i], od_w_out[i], diff_lambda[i], diff_norm[i])
            caches = (cache_k_sb, cache_v_sb, cache_k_diff, cache_v_diff)
            hp, hs, kv_p, kv_s = _odd_mix(hp, hs, mix_norm[l], dims, caches, page_table, i, pos_p, pos_s, freq, prm,
                                          lam_init)
            for n, a_p, a_s in zip(("k_sb", "v_sb", "k_diff", "v_diff"), kv_p, kv_s):
                new_p[n].append(a_p)
                new_s[n].append(a_s)
        final = final_norm if l == depth - 1 else None
        hp = _ffn_half_step(hp, ffn_norm[l, 1], wg[l, 1], wu[l, 1], wd[l, 1], final)
        hs = _ffn_half_step(hs, ffn_norm[l, 1], wg[l, 1], wu[l, 1], wd[l, 1], final)

    y_prompt = hp.reshape(bp, lp, d)
    y_sample = hs.reshape(bs, ls_pad, d)[:, :ls]
    out = [y_prompt, y_sample]
    for n in ("k_sb", "v_sb", "k_diff", "v_diff", "gdn", "gdn_conv", "rwkv", "rwkv_shift"):
        out += [jnp.stack(new_p[n]), jnp.stack(new_s[n])]
    return tuple(out)
```

```python
import functools
import math
import types

import jax
import jax.numpy as jnp
import numpy as np
from jax import lax
from jax.experimental import pallas as pl
from jax.experimental.pallas import tpu as pltpu

F32 = jnp.float32
BF16 = jnp.bfloat16

HEAD_DIM = 64
HEAD_SHIFT = 6
NORM_EPS = 1e-6
L2_EPS = 1e-6
RWKV_GN_EPS = 64e-5
ROPE_DIM = HEAD_DIM // 4
ROPE_THETA = 500000.0
CONV_W = 4
ATTN_SCALE = HEAD_DIM ** -0.5
NEG_BIG = -1e30
SB_DEAD_LOG = -100.0

V7X_VMEM_LIMIT_BYTES = 56 * 1024 * 1024
SUBLANES = 8
LANES = 128


def _cparams(*sem):
    return pltpu.CompilerParams(dimension_semantics=sem, vmem_limit_bytes=V7X_VMEM_LIMIT_BYTES)


def _sigmoid(x):
    return 1.0 / (1.0 + jnp.exp(-x))


def _softplus(x):
    return jnp.maximum(x, 0.0) + jnp.log(1.0 + jnp.exp(-jnp.abs(x)))


def _rms(x, eps=NORM_EPS):
    return x * lax.rsqrt(jnp.mean(x * x, axis=-1, keepdims=True) + eps)


def _dot(a, b):
    return jnp.dot(a.astype(BF16), b.astype(BF16), preferred_element_type=F32)


def _dot_nt(a, b):
    return lax.dot_general(a.astype(BF16), b.astype(BF16), (((1,), (1,)), ((), ())),
                           preferred_element_type=F32)


def _dot_tn(a, b):
    return lax.dot_general(a.astype(BF16), b.astype(BF16), (((0,), (0,)), ((), ())),
                           preferred_element_type=F32)


def _dot_exact(a, b):
    return jnp.dot(a, b, precision=lax.Precision.HIGHEST, preferred_element_type=F32)


def _dot_split(x, m_bf16):
    hi = x.astype(BF16)
    lo = (x - hi.astype(F32)).astype(BF16)
    return (jnp.dot(hi, m_bf16, preferred_element_type=F32)
            + jnp.dot(lo, m_bf16, preferred_element_type=F32))


_NN = (((1,), (0,)), ((), ()))
_NT = (((1,), (1,)), ((), ()))
_TN = (((0,), (0,)), ((), ()))


def _dot3(a, b, dims=_NN):
    a_hi = a.astype(BF16)
    a_lo = (a - a_hi.astype(F32)).astype(BF16)
    b_hi = b.astype(BF16)
    b_lo = (b - b_hi.astype(F32)).astype(BF16)
    dg = lambda x, y: lax.dot_general(x, y, dims, preferred_element_type=F32)
    return dg(a_hi, b_hi) + (dg(a_hi, b_lo) + dg(a_lo, b_hi))


def _iota2(shape, dim):
    return lax.broadcasted_iota(jnp.int32, shape, dim)


def _log2(n):
    assert n > 0 and n & (n - 1) == 0, n
    return n.bit_length() - 1


def _inv_i_minus_many(n_mats, order, _dot=_dot):
    size = n_mats[0].shape[0]
    eye = (_iota2((size, size), 0) == _iota2((size, size), 1)).astype(F32)
    ts = [eye + n for n in n_mats]
    ps = list(n_mats)
    k = 1
    squared = False
    while 2 * k < order:
        if not squared:
            ps = [_dot(p, p) for p in ps]
            squared = True
        else:
            prods = [_dot(jnp.concatenate([p, t], axis=0), p) for p, t in zip(ps, ts)]
            ts = [t + pr[size:] for t, pr in zip(ts, prods)]
            ps = [pr[:size] for pr in prods]
        k *= 2
    if squared:
        ts = [t + _dot(t, p) for t, p in zip(ts, ps)]
    return ts


def _tril_block_diag(rows, chunk):
    sh = _log2(chunk)
    r = _iota2((rows, rows), 0)
    c = _iota2((rows, rows), 1)
    return jnp.where((r >> sh) == (c >> sh), jnp.where(r >= c, 1.0, 0.0), 0.0)


def _pair_rows(x2):
    first = _iota2(x2.shape, 1) < HEAD_DIM
    return jnp.concatenate([jnp.where(first, x2, 0.0), jnp.where(first, 0.0, x2)], axis=0)


def _pair_state(s_h0, s_h1):
    zero = jnp.zeros_like(s_h0)
    return jnp.concatenate([jnp.concatenate([s_h0, zero], axis=1), jnp.concatenate([zero, s_h1], axis=1)], axis=0)


def _pair_col(x_h0, x_h1, rows):
    return jnp.concatenate([jnp.broadcast_to(x_h0, (rows, 1)), jnp.broadcast_to(x_h1, (rows, 1))], axis=0)


def _head_halves_mean(x):
    first = _iota2(x.shape, 1) < HEAD_DIM
    m0 = jnp.sum(jnp.where(first, x, 0.0), axis=-1, keepdims=True)
    m1 = jnp.sum(jnp.where(first, 0.0, x), axis=-1, keepdims=True)
    return jnp.where(first, m0, m1) * (1.0 / HEAD_DIM)


def _waves(chunk_subs):
    by_sub = {}
    for j, sub in enumerate(chunk_subs):
        by_sub.setdefault(sub, []).append(j)
    depth = max(len(v) for v in by_sub.values())
    return [[v[w] for v in by_sub.values() if len(v) > w] for w in range(depth)]


def _ffn_body(h_ref, g_ref, wg_ref, wu_ref, wd_ref, fg_ref, o_ref, xn_ref, acc_ref, *, apply_final):
    j = pl.program_id(1)

    @pl.when(j == 0)
    def _():
        xn_ref[...] = (_rms(h_ref[...]) * g_ref[...]).astype(BF16)
        acc_ref[...] = jnp.zeros_like(acc_ref)

    xn = xn_ref[...]
    gate = jnp.dot(xn, wg_ref[...], preferred_element_type=F32)
    up = jnp.dot(xn, wu_ref[...], preferred_element_type=F32)
    act = (gate * _sigmoid(gate)) * up
    acc_ref[...] += jnp.dot(act.astype(BF16), wd_ref[...], preferred_element_type=F32)

    @pl.when(j == pl.num_programs(1) - 1)
    def _():
        out = h_ref[...] + 0.5 * acc_ref[...]
        if apply_final:
            out = _rms(out) * fg_ref[...]
        o_ref[...] = out


def _pick_tile(n, candidates):
    for c in candidates:
        if n % c == 0:
            return c
    return n


def _ffn_half_step(h, norm_g, wg, wu, wd, final_g=None):
    t, d = h.shape
    f = wg.shape[1]
    tm = _pick_tile(t, (512, 256, 128, 64, 32, 16, 8))
    tf = _pick_tile(f, (1408, 1024, 512, 256, 128))
    apply_final = final_g is not None
    fg = final_g if apply_final else norm_g
    return pl.pallas_call(
        functools.partial(_ffn_body, apply_final=apply_final),
        grid=(t // tm, f // tf),
        in_specs=[
            pl.BlockSpec((tm, d), lambda i, j: (i, 0)),
            pl.BlockSpec((1, d), lambda i, j: (0, 0)),
            pl.BlockSpec((d, tf), lambda i, j: (0, j)),
            pl.BlockSpec((d, tf), lambda i, j: (0, j)),
            pl.BlockSpec((tf, d), lambda i, j: (j, 0)),
            pl.BlockSpec((1, d), lambda i, j: (0, 0)),
        ],
        out_specs=pl.BlockSpec((tm, d), lambda i, j: (i, 0)),
        out_shape=jax.ShapeDtypeStruct((t, d), F32),
        scratch_shapes=[pltpu.VMEM((tm, d), BF16), pltpu.VMEM((tm, d), F32)],
        compiler_params=_cparams("parallel", "arbitrary"),
        name="ffn_half_step",
    )(h, norm_g.reshape(1, d), wg, wu, wd, fg.reshape(1, d))


def _proj_even_body(h_ref, g_ref, w_ref, *o_refs, splits):
    xn = (_rms(h_ref[...]) * g_ref[...]).astype(BF16)
    p = jnp.dot(xn, w_ref[...], preferred_element_type=F32)
    off = 0
    for o_ref, width in zip(o_refs, splits):
        o_ref[...] = p[:, off:off + width]
        off += width


def _proj_even(h, norm_g, w, splits):
    t, d = h.shape
    n = w.shape[1]
    tm = _pick_tile(t, (256, 128, 64, 32, 16, 8))
    return pl.pallas_call(
        functools.partial(_proj_even_body, splits=splits),
        grid=(t // tm,),
        in_specs=[
            pl.BlockSpec((tm, d), lambda i: (i, 0)),
            pl.BlockSpec((1, d), lambda i: (0, 0)),
            pl.BlockSpec((d, n), lambda i: (0, 0)),
        ],
        out_specs=[pl.BlockSpec((tm, s), lambda i: (i, 0)) for s in splits],
        out_shape=[jax.ShapeDtypeStruct((t, s), F32) for s in splits],
        compiler_params=_cparams("parallel"),
        name="proj_even",
    )(h, norm_g.reshape(1, d), w)


def _proj_odd_body(h_ref, g_ref, w_ref, pos_ref, freq_ref, o_ref, *, rope_lo, rope_hi):
    xn = (_rms(h_ref[...]) * g_ref[...]).astype(BF16)
    p = jnp.dot(xn, w_ref[...], preferred_element_type=F32)
    o_ref[...] = p
    ang = pos_ref[...] * freq_ref[...]
    cos, sin = jnp.cos(ang), jnp.sin(ang)
    lane = _iota2(ang.shape, 1) & (HEAD_DIM - 1)
    half = ROPE_DIM // 2
    sin_lo = jnp.where(lane < half, -sin, 0.0)
    sin_hi = jnp.where(lane < half, 0.0, jnp.where(lane < ROPE_DIM, sin, 0.0))
    for c0 in range(rope_lo, rope_hi, LANES):
        x = p[:, c0:c0 + LANES]
        up = pltpu.roll(x, LANES - half, axis=1)
        down = pltpu.roll(x, half, axis=1)
        o_ref[:, c0:c0 + LANES] = x * cos + up * sin_lo + down * sin_hi


def _proj_odd(h, norm_g, w, pos, freq, rope_lo, rope_hi):
    t, d = h.shape
    n = w.shape[1]
    tm = _pick_tile(t, (256, 128, 64, 32, 16, 8))
    return pl.pallas_call(
        functools.partial(_proj_odd_body, rope_lo=rope_lo, rope_hi=rope_hi),
        grid=(t // tm,),
        in_specs=[
            pl.BlockSpec((tm, d), lambda i: (i, 0)),
            pl.BlockSpec((1, d), lambda i: (0, 0)),
            pl.BlockSpec((d, n), lambda i: (0, 0)),
            pl.BlockSpec((tm, 1), lambda i: (i, 0)),
            pl.BlockSpec((1, LANES), lambda i: (0, 0)),
        ],
        out_specs=pl.BlockSpec((tm, n), lambda i: (i, 0)),
        out_shape=jax.ShapeDtypeStruct((t, n), F32),
        compiler_params=_cparams("parallel"),
        name="proj_odd",
    )(h, norm_g.reshape(1, d), w, pos, freq)


def _outproj_body(h_ref, xa_ref, xb_ref, wa_ref, wb_ref, o_ref):
    o_ref[...] = (h_ref[...]
                  + jnp.dot(xa_ref[...].astype(BF16), wa_ref[...], preferred_element_type=F32)
                  + jnp.dot(xb_ref[...].astype(BF16), wb_ref[...], preferred_element_type=F32))


def _outproj(h, xa, xb, wa, wb):
    t, d = h.shape
    ka, kb = xa.shape[1], xb.shape[1]
    tm = _pick_tile(t, (512, 256, 128, 64, 32, 16, 8))
    return pl.pallas_call(
        _outproj_body,
        grid=(t // tm,),
        in_specs=[
            pl.BlockSpec((tm, d), lambda i: (i, 0)),
            pl.BlockSpec((tm, ka), lambda i: (i, 0)),
            pl.BlockSpec((tm, kb), lambda i: (i, 0)),
            pl.BlockSpec((ka, d), lambda i: (0, 0)),
            pl.BlockSpec((kb, d), lambda i: (0, 0)),
        ],
        out_specs=pl.BlockSpec((tm, d), lambda i: (i, 0)),
        out_shape=jax.ShapeDtypeStruct((t, d), F32),
        compiler_params=_cparams("parallel"),
        name="mix_outproj",
    )(h, xa, xb, wa, wb)


def _gdn_body(qkv_ref, z_ref, ab_ref, cst_ref, s0_ref, cw_ref, alog_ref, dtb_ref, og_ref, bd_ref, ex_ref,
              o_ref, cnew_ref, snew_ref,
              xbuf, s_scr, c_s, q_s, k_s, v_s, b_s, g_s,
              *, n_sub, tb, chunk, valid, n_heads, group_chunks):
    blk = pl.program_id(1)
    last_blk = pl.num_programs(1) - 1
    w_all = n_heads * HEAD_DIM
    lo = SUBLANES - (CONV_W - 1)
    c = chunk

    @pl.when(blk == 0)
    def _():
        for s in range(n_sub):
            xbuf[s, lo:SUBLANES, :] = cst_ref[s]
            for pair in range(n_heads // 2):
                s_scr[s, pair] = _pair_state(s0_ref[s, 2 * pair], s0_ref[s, 2 * pair + 1])

    cw = cw_ref[...]
    for s in range(n_sub):
        xbuf[s, SUBLANES:SUBLANES + tb, :] = qkv_ref[s * tb:(s + 1) * tb, :]
        y = xbuf[s, lo:lo + tb, :] * cw[0:1]
        for j in range(1, CONV_W):
            y = y + xbuf[s, lo + j:lo + j + tb, :] * cw[j:j + 1]
        c_s[s * tb:(s + 1) * tb, :] = y * _sigmoid(y)
        xbuf[s, lo:SUBLANES, :] = xbuf[s, lo + valid:SUBLANES + valid, :]

    @pl.when(blk == last_blk)
    def _():
        for s in range(n_sub):
            cnew_ref[s] = xbuf[s, lo:SUBLANES, :]

    bd = bd_ref[...]
    q = c_s[:, 0:w_all]
    k = c_s[:, w_all:2 * w_all]
    q_s[...] = q * lax.rsqrt(_dot_split(q * q, bd) + L2_EPS) * ATTN_SCALE
    k_s[...] = k * lax.rsqrt(_dot_split(k * k, bd) + L2_EPS)
    v_s[...] = c_s[:, 2 * w_all:3 * w_all]

    ab = ab_ref[...]
    g = -jnp.exp(alog_ref[...]) * _softplus(ab + dtb_ref[...])
    beta = _sigmoid(ab)
    if valid < tb:
        live = (_iota2(ab.shape, 0) & (tb - 1)) < valid
        g = jnp.where(live, g, 0.0)
        beta = jnp.where(live, beta, 0.0)
    g_s[...] = g
    b_s[...] = _dot_exact(beta, ex_ref[...])

    c2 = 2 * c
    incl = _tril_block_diag(c2, c) > 0.5
    strict = jnp.logical_and(incl, _iota2((c2, c2), 0) != _iota2((c2, c2), 1))
    og2 = og_ref[...]

    def run_group(base, chunk_subs):
        g_rows = len(chunk_subs) * c
        gc_all = _dot_exact(_tril_block_diag(g_rows, c), g_s[pl.ds(base, g_rows), :])
        gct_all = gc_all.T
        chains = []
        for j, sub in enumerate(chunk_subs):
            rs = pl.ds(base + j * c, c)
            js = slice(j * c, (j + 1) * c)
            last = slice((j + 1) * c - 1, (j + 1) * c)
            for pair in range(n_heads // 2):
                lanes = slice(pair * LANES, (pair + 1) * LANES)
                h0, h1 = 2 * pair, 2 * pair + 1
                ch = types.SimpleNamespace(j=j, sub=sub, pair=pair)
                k2, b2 = k_s[rs, lanes], b_s[rs, lanes]
                ch.k = _pair_rows(k2)
                ch.kb = _pair_rows(k2 * b2)
                ch.q = _pair_rows(q_s[rs, lanes])
                ch.vb = _pair_rows(v_s[rs, lanes] * b2)
                gcol = jnp.concatenate([gc_all[js, h0:h0 + 1], gc_all[js, h1:h1 + 1]], axis=0)
                grow = jnp.concatenate([gct_all[h0:h0 + 1, js], gct_all[h1:h1 + 1, js]], axis=1)
                ch.decay = jnp.where(incl, jnp.exp(jnp.where(incl, gcol - grow, 0.0)), 0.0)
                ch.eg = jnp.exp(gcol)
                gl0, gl1 = gc_all[last, h0:h0 + 1], gc_all[last, h1:h1 + 1]
                ch.k_dec = ch.k * jnp.exp(_pair_col(gl0, gl1, c) - gcol)
                ch.s_scale = jnp.exp(_pair_col(gl0, gl1, HEAD_DIM))
                chains.append(ch)
        prods = [_dot3(jnp.concatenate([ch.kb, ch.q], axis=0), ch.k, _NT) for ch in chains]
        for ch, pr in zip(chains, prods):
            ch.n = -jnp.where(strict, pr[:c2] * ch.decay, 0.0)
            ch.qk = jnp.where(incl, pr[c2:] * ch.decay, 0.0)
        t_invs = _inv_i_minus_many([ch.n for ch in chains], c, _dot3)
        sols = [_dot3(t, jnp.concatenate([ch.vb, ch.kb * ch.eg], axis=1)) for t, ch in zip(t_invs, chains)]
        for ch, sol in zip(chains, sols):
            ch.u, ch.w = sol[:, :LANES], sol[:, LANES:]

        state = {}
        for wave in _waves(chunk_subs):
            cur = [ch for ch in chains if ch.j in wave]
            for ch in cur:
                if (ch.sub, ch.pair) not in state:
                    state[(ch.sub, ch.pair)] = s_scr[ch.sub, ch.pair]
            ws_qs = [_dot3(jnp.concatenate([ch.w, ch.q * ch.eg], axis=0), state[(ch.sub, ch.pair)]) for ch in cur]
            v_news = [ch.u - x[:c2] for ch, x in zip(cur, ws_qs)]
            o_parts = [_dot(ch.qk, vn) for ch, vn in zip(cur, v_news)]
            s_parts = [_dot3(ch.k_dec, vn, _TN) for ch, vn in zip(cur, v_news)]
            for ch, x, op, sp in zip(cur, ws_qs, o_parts, s_parts):
                o_bd = x[c2:] + op
                ch.o = o_bd[:c] + o_bd[c:]
                state[(ch.sub, ch.pair)] = state[(ch.sub, ch.pair)] * ch.s_scale + sp
        for (sub, pair), val in state.items():
            s_scr[sub, pair] = val

        for ch in chains:
            rs = pl.ds(base + ch.j * c, c)
            lanes = slice(ch.pair * LANES, (ch.pair + 1) * LANES)
            z2 = z_ref[rs, lanes]
            o2 = ch.o * lax.rsqrt(_head_halves_mean(ch.o * ch.o) + NORM_EPS) * og2
            o_ref[rs, lanes] = o2 * (z2 * _sigmoid(z2))

    chunks_per_seq = tb // c
    if n_sub > 1 or chunks_per_seq <= group_chunks:
        run_group(0, [s for s in range(n_sub) for _ in range(chunks_per_seq)])
    else:
        g_rows = group_chunks * c

        def step(it, carry):
            run_group(pl.multiple_of(it * g_rows, g_rows), [0] * group_chunks)
            return carry

        lax.fori_loop(0, chunks_per_seq // group_chunks, step, 0)

    @pl.when(blk == last_blk)
    def _():
        for s in range(n_sub):
            for pair in range(n_heads // 2):
                s_bd = s_scr[s, pair]
                snew_ref[s, 2 * pair] = s_bd[:HEAD_DIM, :HEAD_DIM]
                snew_ref[s, 2 * pair + 1] = s_bd[HEAD_DIM:, HEAD_DIM:]


def _gdn(qkv, z, ab, conv_state, s0, conv_w, a_log, dt_bias, out_g, *, seq_rows, valid, chunk, tb, n_sub,
         group_chunks):
    n_seq, n_heads = s0.shape[0], s0.shape[1]
    w_all = n_heads * HEAD_DIM
    n_blk = seq_rows // tb
    assert n_sub == 1 or n_blk == 1
    assert n_seq % n_sub == 0 and (tb // chunk) % group_chunks == 0 or tb // chunk <= group_chunks
    _log2(tb), _log2(chunk)
    lane_h = np.arange(w_all) // HEAD_DIM
    bd = jnp.asarray(lane_h[:, None] == lane_h[None, :], BF16)
    ex = jnp.asarray((np.arange(LANES)[:, None] - n_heads) == lane_h[None, :], F32)
    row128 = lambda x: jnp.zeros((1, LANES), F32).at[0, :n_heads].set(x)
    rows_blk = n_sub * tb
    tok = lambda width: pl.BlockSpec((rows_blk, width), lambda s, b: (s * n_blk + b, 0))
    full = lambda a: pl.BlockSpec(a.shape, lambda s, b: (0,) * a.ndim)
    params = (conv_w, row128(a_log), row128(dt_bias), jnp.tile(out_g, 2).reshape(1, LANES), bd, ex)
    rows = n_seq * seq_rows
    return pl.pallas_call(
        functools.partial(_gdn_body, n_sub=n_sub, tb=tb, chunk=chunk, valid=valid, n_heads=n_heads,
                          group_chunks=group_chunks),
        grid=(n_seq // n_sub, n_blk),
        in_specs=[tok(3 * w_all), tok(w_all), tok(LANES),
                  pl.BlockSpec((n_sub, CONV_W - 1, 3 * w_all), lambda s, b: (s, 0, 0)),
                  pl.BlockSpec((n_sub, n_heads, HEAD_DIM, HEAD_DIM), lambda s, b: (s, 0, 0, 0))]
        + [full(a) for a in params],
        out_specs=[tok(w_all),
                   pl.BlockSpec((n_sub, CONV_W - 1, 3 * w_all), lambda s, b: (s, 0, 0)),
                   pl.BlockSpec((n_sub, n_heads, HEAD_DIM, HEAD_DIM), lambda s, b: (s, 0, 0, 0))],
        out_shape=[jax.ShapeDtypeStruct((rows, w_all), F32),
                   jax.ShapeDtypeStruct(conv_state.shape, F32),
                   jax.ShapeDtypeStruct(s0.shape, F32)],
        scratch_shapes=[pltpu.VMEM((n_sub, SUBLANES + tb, 3 * w_all), F32),
                        pltpu.VMEM((n_sub, n_heads // 2, LANES, LANES), F32),
                        pltpu.VMEM((rows_blk, 3 * w_all), F32),
                        pltpu.VMEM((rows_blk, w_all), F32), pltpu.VMEM((rows_blk, w_all), F32),
                        pltpu.VMEM((rows_blk, w_all), F32), pltpu.VMEM((rows_blk, w_all), F32),
                        pltpu.VMEM((rows_blk, LANES), F32)],
        compiler_params=_cparams("parallel", "arbitrary"),
        name="gdn_mixer",
    )(qkv, z, ab, conv_state, s0, *params)


def _rwkv_body(p_ref, sh_ref, s0_ref, mu_ref, w0_ref, a0_ref, kk_ref, ka_ref, rk_ref, lnw_ref, lnb_ref,
               w2_ref, a2_ref, g2_ref, bd_ref,
               o_ref, shnew_ref, snew_ref,
               xbuf, s_scr, r_s, k_s, v_s, kk_s, a_s, ld_s, gate_s, bonus_s,
               *, n_sub, tb, chunk, valid, n_heads, group_chunks):
    blk = pl.program_id(1)
    last_blk = pl.num_programs(1) - 1
    w_all = n_heads * HEAD_DIM
    c = chunk

    @pl.when(blk == 0)
    def _():
        for s in range(n_sub):
            xbuf[s, SUBLANES - 1:SUBLANES, :] = sh_ref[s]
            for pair in range(n_heads // 2):
                s_scr[s, pair] = _pair_state(s0_ref[s, 2 * pair], s0_ref[s, 2 * pair + 1])

    shifted = []
    for s in range(n_sub):
        xbuf[s, SUBLANES:SUBLANES + tb, :] = p_ref[s * tb:(s + 1) * tb, :]
        shifted.append(xbuf[s, SUBLANES - 1:SUBLANES - 1 + tb, :])
        xbuf[s, SUBLANES - 1:SUBLANES, :] = xbuf[s, SUBLANES - 1 + valid:SUBLANES + valid, :]
    shifted = shifted[0] if n_sub == 1 else jnp.concatenate(shifted, axis=0)

    @pl.when(blk == last_blk)
    def _():
        for s in range(n_sub):
            shnew_ref[s] = xbuf[s, SUBLANES - 1:SUBLANES, :]

    p = p_ref[...]
    xs = p + (shifted - p) * mu_ref[...]
    r = xs[:, 0:w_all]
    k = xs[:, w_all:2 * w_all]
    v = xs[:, 2 * w_all:3 * w_all]
    lo = xs[:, 3 * w_all:3 * w_all + LANES]
    g_lo = xs[:, 3 * w_all + LANES:]
    w_log = -_softplus(-(w0_ref[...] + _dot(jnp.tanh(lo), w2_ref[...]))) - 0.5
    log_decay = -jnp.exp(w_log)
    a = _sigmoid(a0_ref[...] + _dot(lo, a2_ref[...]))
    gate_s[...] = _dot(_sigmoid(g_lo), g2_ref[...])
    bd = bd_ref[...]
    kkx = k * kk_ref[...]
    kk = kkx * lax.rsqrt(_dot_split(kkx * kkx, bd) + L2_EPS)
    k = k * (1.0 + (a - 1.0) * ka_ref[...])
    bonus_s[...] = _dot_split(r * k * rk_ref[...], bd) * v
    if valid < tb:
        live = (_iota2(r.shape, 0) & (tb - 1)) < valid
        log_decay = jnp.where(live, log_decay, 0.0)
        kk = jnp.where(live, kk, 0.0)
        v = jnp.where(live, v, 0.0)
    r_s[...] = r
    k_s[...] = k
    v_s[...] = v
    kk_s[...] = kk
    a_s[...] = a
    ld_s[...] = log_decay

    c2 = 2 * c
    incl = _tril_block_diag(c2, c) > 0.5
    strict = jnp.logical_and(incl, _iota2((c2, c2), 0) != _iota2((c2, c2), 1))

    def run_group(base, chunk_subs):
        g_rows = len(chunk_subs) * c
        rows_g = pl.ds(base, g_rows)
        ld = ld_s[rows_g, :]
        gcum = _dot_exact(_tril_block_diag(g_rows, c), ld)
        e_pos = jnp.exp(gcum)
        e_neg = jnp.exp(-gcum)
        kk_g = kk_s[rows_g, :]
        r_t = r_s[rows_g, :] * e_pos
        b_t = kk_g * jnp.exp(gcum - ld)
        a_t = -(kk_g * a_s[rows_g, :]) * e_neg
        k_t = k_s[rows_g, :] * e_neg
        v_g = v_s[rows_g, :]
        chains = []
        for j, sub in enumerate(chunk_subs):
            js = slice(j * c, (j + 1) * c)
            for pair in range(n_heads // 2):
                lanes = slice(pair * LANES, (pair + 1) * LANES)
                ch = types.SimpleNamespace(j=j, sub=sub, pair=pair)
                ch.b, ch.r, ch.v = _pair_rows(b_t[js, lanes]), _pair_rows(r_t[js, lanes]), _pair_rows(v_g[js, lanes])
                ch.left = jnp.concatenate([ch.b, ch.r], axis=0)
                ch.right = jnp.concatenate([_pair_rows(a_t[js, lanes]), _pair_rows(k_t[js, lanes])], axis=0)
                ch.e_last = e_pos[(j + 1) * c - 1:(j + 1) * c, lanes]
                chains.append(ch)
        ms = [_dot_nt(ch.left, ch.right) for ch in chains]
        for ch, m in zip(chains, ms):
            ch.a_ab = jnp.where(strict, m[:c2, :c2], 0.0)
            ch.a_kb = jnp.where(strict, m[:c2, c2:], 0.0)
            ch.q_ak = jnp.concatenate([jnp.where(incl, m[c2:, :c2], 0.0), jnp.where(incl, m[c2:, c2:], 0.0)], axis=1)
        akvs = [_dot(ch.a_kb, ch.v) for ch in chains]
        t_invs = _inv_i_minus_many([ch.a_ab for ch in chains], c)
        tbs = [_dot(t, jnp.concatenate([ch.b, akv], axis=1)) for t, ch, akv in zip(t_invs, chains, akvs)]
        for ch, x in zip(chains, tbs):
            ch.tb_r = jnp.concatenate([x[:, :LANES], ch.r], axis=0)
            ch.takv = x[:, LANES:]

        state = {}
        for wave in _waves(chunk_subs):
            cur = [ch for ch in chains if ch.j in wave]
            for ch in cur:
                if (ch.sub, ch.pair) not in state:
                    state[(ch.sub, ch.pair)] = s_scr[ch.sub, ch.pair]
            lss = [_dot_nt(ch.tb_r, state[(ch.sub, ch.pair)]) for ch in cur]
            uvs = [jnp.concatenate([x[:c2] + ch.takv, ch.v], axis=0) for ch, x in zip(cur, lss)]
            o_parts = [_dot(ch.q_ak, uv) for ch, uv in zip(cur, uvs)]
            s_parts = [_dot_tn(uv, ch.right * ch.e_last) for ch, uv in zip(cur, uvs)]
            for ch, x, op, sp in zip(cur, lss, o_parts, s_parts):
                o_bd = x[c2:] + op
                o = o_bd[:c] + o_bd[c:]
                d = o - _head_halves_mean(o)
                ch.o = d * lax.rsqrt(_head_halves_mean(d * d) + RWKV_GN_EPS)
                state[(ch.sub, ch.pair)] = state[(ch.sub, ch.pair)] * ch.e_last + sp
        for (sub, pair), val in state.items():
            s_scr[sub, pair] = val

        for ch in chains:
            rs = pl.ds(base + ch.j * c, c)
            lanes = slice(ch.pair * LANES, (ch.pair + 1) * LANES)
            on = ch.o * lnw_ref[:, lanes] + lnb_ref[:, lanes]
            o_ref[rs, lanes] = (on + bonus_s[rs, lanes]) * gate_s[rs, lanes]

    chunks_per_seq = tb // c
    if n_sub > 1 or chunks_per_seq <= group_chunks:
        run_group(0, [s for s in range(n_sub) for _ in range(chunks_per_seq)])
    else:
        g_rows = group_chunks * c

        def step(it, carry):
            run_group(pl.multiple_of(it * g_rows, g_rows), [0] * group_chunks)
            return carry

        lax.fori_loop(0, chunks_per_seq // group_chunks, step, 0)

    @pl.when(blk == last_blk)
    def _():
        for s in range(n_sub):
            for pair in range(n_heads // 2):
                s_bd = s_scr[s, pair]
                snew_ref[s, 2 * pair] = s_bd[:HEAD_DIM, :HEAD_DIM]
                snew_ref[s, 2 * pair + 1] = s_bd[HEAD_DIM:, HEAD_DIM:]


def _rwkv(p, shift_state, s0, mu, w0, w2, a0, a2, g2, k_k, k_a, r_k, ln_w, ln_b, *, seq_rows, valid, chunk, tb,
          n_sub, group_chunks):
    n_seq, n_heads = s0.shape[0], s0.shape[1]
    w_all = n_heads * HEAD_DIM
    width = p.shape[1]
    n_blk = seq_rows // tb
    assert n_sub == 1 or n_blk == 1
    _log2(tb), _log2(chunk)
    lane_h = np.arange(w_all) // HEAD_DIM
    bd = jnp.asarray(lane_h[:, None] == lane_h[None, :], BF16)
    lora = w2.shape[0]
    w2p = jnp.zeros((LANES, w_all), BF16).at[:lora].set(w2.astype(BF16))
    a2p = jnp.zeros((LANES, w_all), BF16).at[lora:lora + a2.shape[0]].set(a2.astype(BF16))
    r1 = lambda x: x.reshape(1, -1)
    params = (r1(mu), r1(w0), r1(a0), r1(k_k), r1(k_a), r1(r_k), r1(ln_w), r1(ln_b),
              w2p, a2p, g2.astype(BF16), bd)
    rows_blk = n_sub * tb
    tok = lambda wd: pl.BlockSpec((rows_blk, wd), lambda s, b: (s * n_blk + b, 0))
    full = lambda arr: pl.BlockSpec(arr.shape, lambda s, b: (0,) * arr.ndim)
    rows = n_seq * seq_rows
    return pl.pallas_call(
        functools.partial(_rwkv_body, n_sub=n_sub, tb=tb, chunk=chunk, valid=valid, n_heads=n_heads,
                          group_chunks=group_chunks),
        grid=(n_seq // n_sub, n_blk),
        in_specs=[tok(width),
                  pl.BlockSpec((n_sub, 1, width), lambda s, b: (s, 0, 0)),
                  pl.BlockSpec((n_sub, n_heads, HEAD_DIM, HEAD_DIM), lambda s, b: (s, 0, 0, 0))]
        + [full(arr) for arr in params],
        out_specs=[tok(w_all),
                   pl.BlockSpec((n_sub, 1, width), lambda s, b: (s, 0, 0)),
                   pl.BlockSpec((n_sub, n_heads, HEAD_DIM, HEAD_DIM), lambda s, b: (s, 0, 0, 0))],
        out_shape=[jax.ShapeDtypeStruct((rows, w_all), F32),
                   jax.ShapeDtypeStruct(shift_state.shape, F32),
                   jax.ShapeDtypeStruct(s0.shape, F32)],
        scratch_shapes=[pltpu.VMEM((n_sub, SUBLANES + tb, width), F32),
                        pltpu.VMEM((n_sub, n_heads // 2, LANES, LANES), F32)]
        + [pltpu.VMEM((rows_blk, w_all), F32) for _ in range(8)],
        compiler_params=_cparams("parallel", "arbitrary"),
        name="rwkv7_mixer",
    )(p, shift_state, s0, *params)


def _sb_prompt_body(q_ref, k_ref, v_ref, u_ref, o_ref, *, tq, tk, n_pairs):
    qi = pl.program_id(2)
    n_diag = tq // tk
    n_h = 2 * n_pairs
    lane = _iota2((tq, LANES), 1)
    q_heads = []
    for pair in range(n_pairs):
        q = q_ref[:, pair * LANES:(pair + 1) * LANES] * ATTN_SCALE
        q_heads += [jnp.where(lane < HEAD_DIM, q, 0.0).astype(BF16),
                    jnp.where(lane >= HEAD_DIM, q, 0.0).astype(BF16)]
    qrow = _iota2((tq, tk), 0)
    kidx = _iota2((tq, tk), 1)
    u = u_ref[...]

    def visit(k0, carry, mask):
        ks = [k_ref[pl.ds(k0, tk), pair * LANES:(pair + 1) * LANES].astype(BF16) for pair in range(n_pairs)]
        vs = [v_ref[pl.ds(k0, tk), pair * LANES:(pair + 1) * LANES].astype(BF16) for pair in range(n_pairs)]
        zs = [lax.dot_general(q_heads[hh], ks[hh // 2], (((1,), (1,)), ((), ())), preferred_element_type=F32)
              for hh in range(n_h)]
        sps = [_softplus(z) for z in zs]
        lss = [-sp if mask is None else jnp.where(mask, -sp, 0.0) for sp in sps]
        cums = [_dot_split(ls, u) for ls in lss]
        ws = []
        for hh in range(n_h):
            w = jnp.exp(zs[hh] - sps[hh] + (carry[2 * hh] + cums[hh]))
            ws.append((w if mask is None else jnp.where(mask, w, 0.0)).astype(BF16))
        pvs = [jnp.dot(ws[hh], vs[hh // 2], preferred_element_type=F32) for hh in range(n_h)]
        new = []
        for hh in range(n_h):
            new += [carry[2 * hh] + jnp.sum(lss[hh], axis=-1, keepdims=True), carry[2 * hh + 1] + pvs[hh]]
        return tuple(new)

    carry = (jnp.zeros((tq, 1), F32), jnp.zeros((tq, LANES), F32)) * n_h
    for d in range(n_diag):
        off = (n_diag - 1 - d) * tk
        k0 = pl.multiple_of(qi * tq + off, tk)
        carry = visit(k0, carry, (kidx + off) < qrow)

    def alive(state):
        top = jnp.max(state[1])
        for hh in range(1, n_h):
            top = jnp.maximum(top, jnp.max(state[1 + 2 * hh]))
        return jnp.logical_and(state[0] < qi * n_diag, top > SB_DEAD_LOG)

    def body(state):
        jj = state[0]
        k0 = pl.multiple_of((qi * n_diag - 1 - jj) * tk, tk)
        return (jj + 1,) + visit(k0, state[1:], None)

    res = lax.while_loop(alive, body, (jnp.int32(0),) + carry)
    for pair in range(n_pairs):
        o_ref[:, pair * LANES:(pair + 1) * LANES] = jnp.where(lane < HEAD_DIM, res[2 + 4 * pair], res[4 + 4 * pair])


def _sb_prompt(p, n_batch, seq, col_q, col_k, col_v, n_heads):
    tq = _pick_tile(seq, (256, 128, 64, 32, 16, 8))
    tk = tq
    n_q = seq // tq
    n_pairs = 2 if n_heads % 4 == 0 else 1
    wblk = n_pairs * LANES
    n_grp = n_heads * HEAD_DIM // wblk
    u = jnp.asarray(np.arange(tk)[:, None] > np.arange(tk)[None, :], BF16)
    return pl.pallas_call(
        functools.partial(_sb_prompt_body, tq=tq, tk=tk, n_pairs=n_pairs),
        grid=(n_batch, n_grp, n_q),
        in_specs=[
            pl.BlockSpec((tq, wblk), lambda b, g, i: (b * n_q + i, col_q // wblk + g)),
            pl.BlockSpec((seq, wblk), lambda b, g, i: (b, col_k // wblk + g)),
            pl.BlockSpec((seq, wblk), lambda b, g, i: (b, col_v // wblk + g)),
            pl.BlockSpec((tk, tk), lambda b, g, i: (0, 0)),
        ],
        out_specs=pl.BlockSpec((tq, wblk), lambda b, g, i: (b * n_q + i, g)),
        out_shape=jax.ShapeDtypeStruct((n_batch * seq, n_heads * HEAD_DIM), F32),
        compiler_params=_cparams("parallel", "parallel", "arbitrary"),
        name="sb_attn_prompt",
    )(p, p, p, u)


def _lambda_of(lam_ref, lam_init):
    lv = lam_ref[...]
    return (jnp.exp(jnp.sum(lv[0:1] * lv[1:2], axis=-1, keepdims=True))
            - jnp.exp(jnp.sum(lv[2:3] * lv[3:4], axis=-1, keepdims=True)) + lam_init)


def _diff_prompt_body(q_ref, k_ref, v_ref, lam_ref, sn_ref, o_ref, *, tq, tk, lam_init):
    qi = pl.program_id(2)
    lane = _iota2((tq, LANES), 1)
    q = q_ref[...] * ATTN_SCALE
    q_maps = [jnp.where(lane < HEAD_DIM, q, 0.0).astype(BF16),
              jnp.where(lane >= HEAD_DIM, q, 0.0).astype(BF16)]

    def visit(k0, carry, mask):
        k = k_ref[pl.ds(k0, tk), :].astype(BF16)
        v = v_ref[pl.ds(k0, tk), :].astype(BF16)
        ss = [lax.dot_general(qm, k, (((1,), (1,)), ((), ())), preferred_element_type=F32) for qm in q_maps]
        if mask is not None:
            ss = [jnp.where(mask, s, NEG_BIG) for s in ss]
        m_news = [jnp.maximum(carry[3 * mm], jnp.max(ss[mm], axis=-1, keepdims=True)) for mm in range(2)]
        prs = [jnp.exp(ss[mm] - m_news[mm]) for mm in range(2)]
        if mask is not None:
            prs = [jnp.where(mask, pr, 0.0) for pr in prs]
        pvs = [jnp.dot(pr.astype(BF16), v, preferred_element_type=F32) for pr in prs]
        new = []
        for mm in range(2):
            m_run, l_run, acc = carry[3 * mm:3 * mm + 3]
            corr = jnp.exp(m_run - m_news[mm])
            new += [m_news[mm], l_run * corr + jnp.sum(prs[mm], axis=-1, keepdims=True), acc * corr + pvs[mm]]
        return tuple(new)

    one = (jnp.full((tq, 1), NEG_BIG, F32), jnp.zeros((tq, 1), F32), jnp.zeros((tq, LANES), F32))
    n_full = (qi * tq) // tk
    res = lax.fori_loop(0, n_full, lambda kb, cr: visit(pl.multiple_of(kb * tk, tk), cr, None), one * 2)
    k0 = pl.multiple_of(n_full * tk, tk)
    mask = (k0 + _iota2((tq, tk), 1)) <= (qi * tq + _iota2((tq, tk), 0))
    res = visit(k0, res, mask)
    lam = _lambda_of(lam_ref, lam_init)
    o = res[2] / res[1] - lam * (res[5] / res[4])
    o_ref[...] = _rms(o) * sn_ref[...] * (1.0 - lam_init)


def _diff_prompt(p, n_batch, seq, col_q, col_k, col_v, n_heads, lam_vecs, sub_norm, lam_init):
    tq = _pick_tile(seq, (512, 256, 128, 64, 32, 16, 8))
    tk = _pick_tile(seq, (512, 256, 128, 64, 32, 16, 8))
    assert tk % tq == 0
    n_q = seq // tq
    col_q, col_k, col_v = col_q // LANES, col_k // LANES, col_v // LANES
    return pl.pallas_call(
        functools.partial(_diff_prompt_body, tq=tq, tk=tk, lam_init=lam_init),
        grid=(n_batch, n_heads, n_q),
        in_specs=[
            pl.BlockSpec((tq, LANES), lambda b, h, i: (b * n_q + i, col_q + h)),
            pl.BlockSpec((seq, LANES), lambda b, h, i: (b, col_k + h)),
            pl.BlockSpec((seq, LANES), lambda b, h, i: (b, col_v + h)),
            pl.BlockSpec(lam_vecs.shape, lambda b, h, i: (0, 0)),
            pl.BlockSpec((1, LANES), lambda b, h, i: (0, 0)),
        ],
        out_specs=pl.BlockSpec((tq, LANES), lambda b, h, i: (b * n_q + i, h)),
        out_shape=jax.ShapeDtypeStruct((n_batch * seq, n_heads * LANES), F32),
        compiler_params=_cparams("parallel", "parallel", "arbitrary"),
        name="diff_attn_prompt",
    )(p, p, p, lam_vecs, sub_norm.reshape(1, LANES))


def _block_diag_queries(q, n_groups):
    lane_g = _iota2(q.shape, 1) >> HEAD_SHIFT
    return jnp.concatenate([jnp.where(lane_g == g, q, 0.0) for g in range(n_groups)], axis=0)


def _sb_sample_body(pt_ref, q_ref, kn_ref, vn_ref, *rest, n_new, n_groups, n_pages):
    del pt_ref
    kt_refs, vt_refs = rest[:n_pages], rest[n_pages:2 * n_pages]
    u_ref, o_ref = rest[2 * n_pages:]
    rows = n_groups * SUBLANES
    width = n_groups * HEAD_DIM
    qrow = _iota2((rows, 1), 0) & (SUBLANES - 1)

    qbd = _block_diag_queries(q_ref[...] * ATTN_SCALE, n_groups)
    c = jnp.zeros((rows, 1), F32)
    acc = jnp.zeros((rows, width), F32)
    kn = kn_ref[...]
    vn = vn_ref[...]
    for t in range(n_new - 1, -1, -1):
        z = jnp.sum(qbd * kn[t:t + 1, :], axis=-1, keepdims=True)
        vis = t < qrow
        sp = _softplus(z)
        acc = acc + jnp.where(vis, jnp.exp(z - sp + c), 0.0) * vn[t:t + 1, :]
        c = c + jnp.where(vis, -sp, 0.0)

    page = lambda x, j: x[:, j * LANES:(j + 1) * LANES]
    kt_all = jnp.concatenate([r[...].reshape(width, LANES).astype(BF16) for r in kt_refs], axis=1)
    vt_all = jnp.concatenate([r[...].reshape(width, LANES).astype(BF16) for r in vt_refs], axis=1)
    z = jnp.dot(qbd.astype(BF16), kt_all, preferred_element_type=F32)
    sp = _softplus(z)
    sp_rows = jnp.concatenate([page(sp, j) for j in range(n_pages)], axis=0)
    cum_rows = _dot_split(sp_rows, u_ref[...])
    tot_rows = jnp.sum(sp_rows, axis=-1, keepdims=True)
    ws = [None] * n_pages
    for j in range(n_pages - 1, -1, -1):
        rj = slice(j * rows, (j + 1) * rows)
        ws[j] = jnp.exp(page(z, j) - page(sp, j) + (c - cum_rows[rj, :]))
        c = c - tot_rows[rj, :]
    acc = acc + lax.dot_general(jnp.concatenate(ws, axis=1).astype(BF16), vt_all, (((1,), (1,)), ((), ())),
                                preferred_element_type=F32)

    lane_g = _iota2((SUBLANES, width), 1) >> HEAD_SHIFT
    out = jnp.zeros((SUBLANES, width), F32)
    for g in range(n_groups):
        out = out + jnp.where(lane_g == g, acc[g * SUBLANES:(g + 1) * SUBLANES, :], 0.0)
    o_ref[...] = out


def _sb_sample(page_table, p3, col0, cache_kt, cache_vt, layer, n_new):
    n_b, n_pages = page_table.shape
    n_heads = cache_kt.shape[2]
    width = n_heads * HEAD_DIM
    page = cache_kt.shape[-1]
    u = jnp.asarray(np.arange(page)[:, None] > np.arange(page)[None, :], BF16)
    tok = lambda c: pl.BlockSpec((None, SUBLANES, width), lambda b, pt: (b, 0, c))
    pg = [pl.BlockSpec((None, None, n_heads, HEAD_DIM, page), lambda b, pt, j=j: (layer, pt[b, j], 0, 0, 0))
          for j in range(n_pages)]
    return pl.pallas_call(
        functools.partial(_sb_sample_body, n_new=n_new, n_groups=n_heads, n_pages=n_pages),
        grid_spec=pltpu.PrefetchScalarGridSpec(
            num_scalar_prefetch=1,
            grid=(n_b,),
            in_specs=[tok(col0), tok(col0 + 1), tok(col0 + 2)] + pg + pg
            + [pl.BlockSpec((page, page), lambda b, pt: (0, 0))],
            out_specs=tok(0)),
        out_shape=jax.ShapeDtypeStruct((n_b, SUBLANES, width), F32),
        compiler_params=_cparams("parallel"),
        name="sb_attn_sample",
    )(page_table, p3, p3, p3, *([cache_kt] * n_pages), *([cache_vt] * n_pages), u)


def _diff_sample_body(pt_ref, q_ref, kn_ref, vn_ref, *rest, n_new, n_heads, n_pages, lam_init):
    del pt_ref
    kt_refs, v_refs = rest[:n_pages], rest[n_pages:2 * n_pages]
    ex_ref, lam_ref, sn_ref, o_ref = rest[2 * n_pages:]
    n_groups = 2 * n_heads
    rows = n_groups * SUBLANES
    hrows = 2 * SUBLANES
    qrow = _iota2((rows, 1), 0) & (SUBLANES - 1)

    qbd = _block_diag_queries(q_ref[...] * ATTN_SCALE, n_groups)
    kn = kn_ref[...]
    vn = vn_ref[...]
    m_run = jnp.full((rows, 1), NEG_BIG, F32)
    l_run = jnp.zeros((rows, 1), F32)
    acc = jnp.zeros((rows, LANES), F32)
    for t in range(n_new):
        s = jnp.sum(qbd * kn[t:t + 1, :], axis=-1, keepdims=True)
        vis = t <= qrow
        m_new = jnp.maximum(m_run, jnp.where(vis, s, NEG_BIG))
        corr = jnp.exp(m_run - m_new)
        pr = jnp.where(vis, jnp.exp(s - m_new), 0.0)
        v_rows = jnp.concatenate(
            [jnp.broadcast_to(vn[t:t + 1, h * LANES:(h + 1) * LANES], (hrows, LANES)) for h in range(n_heads)],
            axis=0)
        l_run = l_run * corr + pr
        acc = acc * corr + pr * v_rows
        m_run = m_new

    kt_all = jnp.concatenate([r[...].reshape(n_groups * HEAD_DIM, LANES).astype(BF16) for r in kt_refs], axis=1)
    s = jnp.dot(qbd.astype(BF16), kt_all, preferred_element_type=F32)
    m_new = jnp.maximum(m_run, jnp.max(s, axis=-1, keepdims=True))
    corr = jnp.exp(m_run - m_new)
    pr = jnp.exp(s - m_new)
    l_run = l_run * corr + jnp.sum(pr, axis=-1, keepdims=True)
    pr_rows = jnp.concatenate([pr[:, j * LANES:(j + 1) * LANES] for j in range(n_pages)], axis=0)
    spread = jnp.dot(pr_rows.astype(BF16), ex_ref[...], preferred_element_type=F32)
    own = ((_iota2((rows, LANES * n_heads), 1) & (n_heads - 1))
           == (_iota2((rows, LANES * n_heads), 0) >> _log2(hrows)))
    pr_exp = jnp.concatenate(
        [jnp.where(own, spread[j * rows:(j + 1) * rows, :], 0.0).astype(BF16) for j in range(n_pages)], axis=1)
    v_all = jnp.concatenate([r[...].astype(BF16) for r in v_refs], axis=0)
    o_all = (acc * corr + jnp.dot(pr_exp, v_all, preferred_element_type=F32)) / l_run
    lam = _lambda_of(lam_ref, lam_init)
    outs = []
    for h in range(n_heads):
        o = o_all[h * hrows:h * hrows + SUBLANES, :] - lam * o_all[h * hrows + SUBLANES:(h + 1) * hrows, :]
        outs.append(_rms(o) * sn_ref[...] * (1.0 - lam_init))
    o_ref[...] = jnp.concatenate(outs, axis=1)


def _diff_sample(page_table, p3, col0, cache_kt, cache_v, layer, n_new, lam_vecs, sub_norm, lam_init):
    n_b, n_pages = page_table.shape
    n_heads = cache_kt.shape[2]
    page = cache_kt.shape[-1]
    width = n_heads * 2 * HEAD_DIM
    tok = lambda c: pl.BlockSpec((None, SUBLANES, width), lambda b, pt: (b, 0, c))
    kpg = [pl.BlockSpec((None, None, n_heads, 2, HEAD_DIM, page), lambda b, pt, j=j: (layer, pt[b, j], 0, 0, 0, 0))
           for j in range(n_pages)]
    vpg = [pl.BlockSpec((None, None, page * n_heads, LANES), lambda b, pt, j=j: (layer, pt[b, j], 0, 0))
           for j in range(n_pages)]
    _log2(n_heads)
    ex = jnp.asarray(np.arange(page)[:, None] == (np.arange(page * n_heads)[None, :] // n_heads), BF16)
    cache_v2 = cache_v.reshape(cache_v.shape[:2] + (page * n_heads, LANES))
    return pl.pallas_call(
        functools.partial(_diff_sample_body, n_new=n_new, n_heads=n_heads, n_pages=n_pages, lam_init=lam_init),
        grid_spec=pltpu.PrefetchScalarGridSpec(
            num_scalar_prefetch=1,
            grid=(n_b,),
            in_specs=[tok(col0), tok(col0 + 1), tok(col0 + 2)] + kpg + vpg
            + [pl.BlockSpec(ex.shape, lambda b, pt: (0, 0)),
               pl.BlockSpec(lam_vecs.shape, lambda b, pt: (0, 0)), pl.BlockSpec((1, LANES), lambda b, pt: (0, 0))],
            out_specs=tok(0)),
        out_shape=jax.ShapeDtypeStruct((n_b, SUBLANES, width), F32),
        compiler_params=_cparams("parallel"),
        name="diff_attn_sample",
    )(page_table, p3, p3, p3, *([cache_kt] * n_pages), *([cache_v2] * n_pages),
      ex, lam_vecs, sub_norm.reshape(1, LANES))


def _even_mix(hp, hs, norm_g, dims, states, prm):
    bp, lp, bs, ls, ls_pad = dims
    (w_in, w_out, conv_w, a_log, dt_bias, out_g, mu, w0, w2, a0, a2, g2, k_k, k_a, r_k, ln_w, ln_b) = prm
    conv_s, gdn_s, shift_s, rwkv_s = states
    n_heads = a_log.shape[0]
    w_gdn = n_heads * HEAD_DIM
    gdn_in = 4 * w_gdn + 2 * n_heads
    ab_w = jnp.zeros((w_in.shape[0], LANES), F32).at[:, :2 * n_heads].set(w_in[:, 4 * w_gdn:gdn_in])
    w_cat = jnp.concatenate([w_in[:, :4 * w_gdn], w_in[:, gdn_in:], ab_w], axis=1).astype(BF16)
    rwkv_in = w_in.shape[1] - gdn_in
    splits = (3 * w_gdn, w_gdn, rwkv_in, LANES)
    w_out_b = w_out.astype(BF16)
    gdn_prm = (conv_w, a_log, dt_bias, out_g)
    rwkv_prm = (mu, w0, w2, a0, a2, g2, k_k, k_a, r_k, ln_w, ln_b)

    chunk_p = min(64, lp)
    tb_p = _pick_tile(lp, (256, 128, 64)) if lp >= 64 else lp
    cfg_p = dict(seq_rows=lp, valid=tb_p, chunk=chunk_p, tb=tb_p, n_sub=1, group_chunks=4)
    zeros = lambda shape: jnp.zeros(shape, F32)
    qkv, z, p_rwkv, ab = _proj_even(hp, norm_g, w_cat, splits)
    o_a, conv_p, gdn_p = _gdn(qkv, z, ab, zeros((bp,) + conv_s.shape[1:]), zeros((bp,) + gdn_s.shape[1:]),
                              *gdn_prm, **cfg_p)
    o_b, shift_p, rwkv_p = _rwkv(p_rwkv, zeros((bp,) + shift_s.shape[1:]), zeros((bp,) + rwkv_s.shape[1:]),
                                 *rwkv_prm, **cfg_p)
    hp = _outproj(hp, o_a, o_b, w_out_b[:w_gdn], w_out_b[w_gdn:])

    cfg_s = dict(seq_rows=ls_pad, valid=ls, chunk=ls_pad, tb=ls_pad, n_sub=_pick_tile(bs, (4, 2, 1)),
                 group_chunks=1)
    qkv, z, p_rwkv, ab = _proj_even(hs, norm_g, w_cat, splits)
    o_a, conv_n, gdn_n = _gdn(qkv, z, ab, conv_s, gdn_s, *gdn_prm, **cfg_s)
    o_b, shift_n, rwkv_n = _rwkv(p_rwkv, shift_s, rwkv_s, *rwkv_prm, **cfg_s)
    hs = _outproj(hs, o_a, o_b, w_out_b[:w_gdn], w_out_b[w_gdn:])
    return hp, hs, (conv_p, gdn_p, shift_p, rwkv_p), (conv_n, gdn_n, shift_n, rwkv_n)


def _odd_mix(hp, hs, norm_g, dims, caches, page_table, layer_idx, pos_p, pos_s, freq, prm, lam_init):
    bp, lp, bs, ls, ls_pad = dims
    assert ls_pad == SUBLANES
    w_in, w_out, lam_vecs, sub_norm = prm
    ck_sb, cv_sb, ck_d, cv_d = caches
    h_sb = ck_sb.shape[3]
    h_d = ck_d.shape[3]
    w_sb = h_sb * HEAD_DIM
    w_d = h_d * 2 * HEAD_DIM
    assert w_sb == w_d
    d0 = 3 * w_sb
    w_in_b = w_in.astype(BF16)
    w_out_b = w_out.astype(BF16)

    pp = _proj_odd(hp, norm_g, w_in_b, pos_p, freq, d0, d0 + 2 * w_d)
    o_sb = _sb_prompt(pp, bp, lp, 0, w_sb, 2 * w_sb, h_sb)
    o_d = _diff_prompt(pp, bp, lp, d0, d0 + w_d, d0 + 2 * w_d, h_d, lam_vecs, sub_norm, lam_init)
    hp = _outproj(hp, o_sb, o_d, w_out_b[:w_sb], w_out_b[w_sb:])

    ps = _proj_odd(hs, norm_g, w_in_b, pos_s, freq, d0, d0 + 2 * w_d)
    ps3 = ps.reshape(bs, ls_pad, ps.shape[-1])
    o_sb = _sb_sample(page_table, ps3, 0, jnp.transpose(ck_sb, (0, 1, 3, 4, 2)),
                      jnp.transpose(cv_sb, (0, 1, 3, 4, 2)), layer_idx, ls)
    o_d = _diff_sample(page_table, ps3, d0 // w_d, jnp.transpose(ck_d, (0, 1, 3, 4, 5, 2)), cv_d, layer_idx, ls,
                       lam_vecs, sub_norm, lam_init)
    hs = _outproj(hs, o_sb.reshape(bs * ls_pad, w_sb), o_d.reshape(bs * ls_pad, w_d),
                  w_out_b[:w_sb], w_out_b[w_sb:])

    def split(rows):
        b, l = rows.shape[:2]
        k_sb = rows[..., w_sb:2 * w_sb].reshape(b, l, h_sb, HEAD_DIM)
        v_sb = rows[..., 2 * w_sb:3 * w_sb].reshape(b, l, h_sb, HEAD_DIM)
        k_d = rows[..., d0 + w_d:d0 + 2 * w_d].reshape(b, l, h_d, 2, HEAD_DIM)
        v_d = rows[..., d0 + 2 * w_d:d0 + 3 * w_d].reshape(b, l, h_d, 2 * HEAD_DIM)
        return k_sb, v_sb, k_d, v_d

    return hp, hs, split(pp.reshape(bp, lp, pp.shape[-1])), split(ps3[:, :ls])


def kernel(x_prompt, x_sample, cache_k_sb, cache_v_sb, cache_k_diff, cache_v_diff, page_table, state_gdn, state_gdn_conv, state_rwkv, state_rwkv_shift, ffn_norm, ffn_w_gate, ffn_w_up, ffn_w_down, mix_norm, final_norm, ev_w_in, ev_w_out, gdn_conv_w, gdn_a_log, gdn_dt_bias, gdn_out_norm, rwkv_mu, rwkv_w0, rwkv_w2, rwkv_a0, rwkv_a2, rwkv_g2, rwkv_k_k, rwkv_k_a, rwkv_r_k, rwkv_ln_w, rwkv_ln_b, od_w_in, od_w_out, diff_lambda, diff_norm):
    bp, lp, d = x_prompt.shape
    bs, ls, _ = x_sample.shape
    depth = ffn_norm.shape[0]
    ls_pad = -(-ls // SUBLANES) * SUBLANES
    dims = (bp, lp, bs, ls, ls_pad)
    past_len = page_table.shape[1] * cache_k_sb.shape[2]

    hp = x_prompt.reshape(bp * lp, d)
    hs = jnp.pad(x_sample, ((0, 0), (0, ls_pad - ls), (0, 0))).reshape(bs * ls_pad, d)
    pos_p = jnp.tile(jnp.arange(lp), bp).astype(F32).reshape(-1, 1)
    pos_s = jnp.tile(past_len + jnp.arange(ls_pad), bs).astype(F32).reshape(-1, 1)
    half = ROPE_DIM // 2
    inv_freq = ROPE_THETA ** (-jnp.arange(half, dtype=F32) * 2.0 / ROPE_DIM)
    lane = np.arange(LANES) % HEAD_DIM
    freq = jnp.where(lane < ROPE_DIM, inv_freq[lane % half], 0.0).astype(F32).reshape(1, LANES)

    wg, wu, wd = (w.astype(BF16) for w in (ffn_w_gate, ffn_w_up, ffn_w_down))
    names = ("k_sb", "v_sb", "k_diff", "v_diff", "gdn_conv", "gdn", "rwkv_shift", "rwkv")
    new_p = {n: [] for n in names}
    new_s = {n: [] for n in names}
    for l in range(depth):
        i = l // 2
        hp = _ffn_half_step(hp, ffn_norm[l, 0], wg[l, 0], wu[l, 0], wd[l, 0])
        hs = _ffn_half_step(hs, ffn_norm[l, 0], wg[l, 0], wu[l, 0], wd[l, 0])
        if l % 2 == 0:
            prm = (ev_w_in[i], ev_w_out[i], gdn_conv_w[i], gdn_a_log[i], gdn_dt_bias[i], gdn_out_norm[i],
                   rwkv_mu[i], rwkv_w0[i], rwkv_w2[i], rwkv_a0[i], rwkv_a2[i], rwkv_g2[i],
                   rwkv_k_k[i], rwkv_k_a[i], rwkv_r_k[i], rwkv_ln_w[i], rwkv_ln_b[i])
            states = (state_gdn_conv[i], state_gdn[i], state_rwkv_shift[i], state_rwkv[i])
            hp, hs, st_p, st_s = _even_mix(hp, hs, mix_norm[l], dims, states, prm)
            for n, a_p, a_s in zip(("gdn_conv", "gdn", "rwkv_shift", "rwkv"), st_p, st_s):
                new_p[n].append(a_p)
                new_s[n].append(a_s)
        else:
            lam_init = 0.8 - 0.6 * math.exp(-0.3 * l)
            prm = (od_w_in[i], od_w_out[i], diff_lambda[i], diff_norm[i])
            caches = (cache_k_sb, cache_v_sb, cache_k_diff, cache_v_diff)
            hp, hs, kv_p, kv_s = _odd_mix(hp, hs, mix_norm[l], dims, caches, page_table, i, pos_p, pos_s, freq, prm,
                                          lam_init)
            for n, a_p, a_s in zip(("k_sb", "v_sb", "k_diff", "v_diff"), kv_p, kv_s):
                new_p[n].append(a_p)
                new_s[n].append(a_s)
        final = final_norm if l == depth - 1 else None
        hp = _ffn_half_step(hp, ffn_norm[l, 1], wg[l, 1], wu[l, 1], wd[l, 1], final)
        hs = _ffn_half_step(hs, ffn_norm[l, 1], wg[l, 1], wu[l, 1], wd[l, 1], final)

    y_prompt = hp.reshape(bp, lp, d)
    y_sample = hs.reshape(bs, ls_pad, d)[:, :ls]
    out = [y_prompt, y_sample]
    for n in ("k_sb", "v_sb", "k_diff", "v_diff", "gdn", "gdn_conv", "rwkv", "rwkv_shift"):
        out += [jnp.stack(new_p[n]), jnp.stack(new_s[n])]
    return tuple(out)
```
